```python
import jax, jax.numpy as jnp
from jax import lax
import numpy as np

D_MODEL = 1024
BATCH = 8
SEQ = 2048
DEPTH = 2

CHUNK = 64
D_INNER = 2 * D_MODEL

SB_WIDTH = D_INNER // 4
SB_HEAD_DIM = 64
SB_HEADS = SB_WIDTH // SB_HEAD_DIM
SB_BLOCK = 128

SSD_WIDTH = D_INNER // 2
SSD_HEAD_DIM = 64
SSD_HEADS = SSD_WIDTH // SSD_HEAD_DIM
SSD_GROUPS = 2
SSD_STATE = 128
SSD_CONV = 4
SSD_CONV_DIM = SSD_WIDTH + 2 * SSD_GROUPS * SSD_STATE

GLA_WIDTH = D_INNER // 4
GLA_HEADS = 4
GLA_KEY = GLA_WIDTH // 2
GLA_HEAD_K = GLA_KEY // GLA_HEADS
GLA_HEAD_V = GLA_WIDTH // GLA_HEADS
GLA_GATE_RANK = 16
GLA_GATE_TAU = 16.0

NORM_EPS = 1e-6

IN_SPLITS = (SB_WIDTH, SB_WIDTH, SB_WIDTH, SB_WIDTH,
             SSD_WIDTH, SSD_CONV_DIM, SSD_HEADS,
             GLA_KEY, GLA_KEY, GLA_WIDTH, GLA_WIDTH, GLA_GATE_RANK)
D_IN_PROJ = 4 * SB_WIDTH + SSD_WIDTH + SSD_CONV_DIM + SSD_HEADS + 2 * GLA_KEY + 2 * GLA_WIDTH + GLA_GATE_RANK

kernel_name = "hymba_style_sb_ssd_gla_trunk"


def _rms_norm(u, w):
    uf = u.astype(jnp.float32)
    y = uf * lax.rsqrt(jnp.mean(uf * uf, axis=-1, keepdims=True) + NORM_EPS)
    return y * w.astype(jnp.float32)


def _split_cols(u, sizes):
    outs = []
    start = 0
    for n in sizes:
        outs.append(u[..., start:start + n])
        start += n
    return outs


def _stick_breaking(q, k, v):
    b, s, h, d = q.shape
    scale = d ** -0.5
    outs = []
    for i in range(s // SB_BLOCK):
        q0 = i * SB_BLOCK
        kend = q0 + SB_BLOCK
        z = jnp.einsum("bthd,bshd->bhts", q[:, q0:kend], k[:, :kend]) * scale
        t_idx = q0 + jnp.arange(SB_BLOCK)
        s_idx = jnp.arange(kend)
        causal = s_idx[None, :] < t_idx[:, None]
        log_beta = jax.nn.log_sigmoid(z)
        log_rest = jnp.where(causal, jax.nn.log_sigmoid(-z), 0.0)
        after = lax.cumsum(log_rest, axis=3, reverse=True) - log_rest
        a = jnp.where(causal, jnp.exp(log_beta + after), 0.0)
        outs.append(jnp.einsum("bhts,bshd->bthd", a, v[:, :kend]))
    return jnp.concatenate(outs, axis=1)


def _causal_depthwise_conv(u, w, bias):
    c = u.shape[-1]
    out = lax.conv_general_dilated(
        u, w[:, None, :], window_strides=(1,), padding=[(w.shape[0] - 1, 0)],
        dimension_numbers=("NWC", "WIO", "NWC"), feature_group_count=c)
    return out + bias


def _ssd(xbc_raw, dt_raw, z, conv_w, conv_b, dt_bias, a_log, d_skip, norm_w):
    b, s, _ = xbc_raw.shape
    G, R, P, N = SSD_GROUPS, SSD_HEADS // SSD_GROUPS, SSD_HEAD_DIM, SSD_STATE
    c, l = s // CHUNK, CHUNK
    xbc = jax.nn.silu(_causal_depthwise_conv(xbc_raw, conv_w.astype(jnp.float32), conv_b.astype(jnp.float32)))
    xs, bm, cm = _split_cols(xbc, (SSD_WIDTH, G * N, G * N))
    x = xs.reshape(b, c, l, G, R, P)
    bm = bm.reshape(b, c, l, G, N)
    cm = cm.reshape(b, c, l, G, N)
    dt = jax.nn.softplus(dt_raw + dt_bias.astype(jnp.float32)).reshape(b, c, l, G, R)
    a = dt * (-jnp.exp(a_log.astype(jnp.float32))).reshape(G, R)
    a_cs = jnp.cumsum(a, axis=2)
    seg = a_cs[:, :, :, None] - a_cs[:, :, None, :]
    mask = (jnp.arange(l)[:, None] >= jnp.arange(l)[None, :])[None, None, :, :, None, None]
    decay = jnp.exp(jnp.where(mask, seg, -jnp.inf))
    scores = jnp.einsum("bclgn,bcmgn->bclmg", cm, bm)
    wts = scores[..., None] * decay * dt[:, :, None]
    y_diag = jnp.einsum("bclmgr,bcmgrp->bclgrp", wts, x)
    decay_end = jnp.exp(a_cs[:, :, -1:] - a_cs) * dt
    states = jnp.einsum("bclgn,bclgr,bclgrp->bcgrpn", bm, decay_end, x)
    chunk_decay = jnp.exp(a_cs[:, :, -1])

    def step(h, inp):
        st, dc = inp
        return dc[..., None, None] * h + st, h

    h0 = jnp.zeros((b, G, R, P, N), jnp.float32)
    _, prev = lax.scan(step, h0, (jnp.moveaxis(states, 1, 0), jnp.moveaxis(chunk_decay, 1, 0)))
    prev = jnp.moveaxis(prev, 0, 1)
    y_off = jnp.einsum("bclgn,bcgrpn,bclgr->bclgrp", cm, prev, jnp.exp(a_cs))
    y = y_diag + y_off + d_skip.astype(jnp.float32).reshape(G, R)[:, :, None] * x
    y = y.reshape(b, s, SSD_WIDTH) * jax.nn.silu(z)
    y = _rms_norm(y.reshape(b, s, G, SSD_WIDTH // G), jnp.ones((), jnp.float32)).reshape(b, s, SSD_WIDTH)
    return y * norm_w.astype(jnp.float32)


def _gla(q, k, v, g_lr, gate_w, gate_b, norm_w):
    b, s, _ = q.shape
    H, K, V = GLA_HEADS, GLA_HEAD_K, GLA_HEAD_V
    c, l = s // CHUNK, CHUNK
    log_g = jax.nn.log_sigmoid(g_lr @ gate_w.astype(jnp.float32) + gate_b.astype(jnp.float32)) / GLA_GATE_TAU

    def chunks(u, d):
        return jnp.moveaxis(u.reshape(b, c, l, H, d), 1, 0)

    qc = chunks(q * K ** -0.5, K)
    kc = chunks(k, K)
    vc = chunks(v, V)
    gc = chunks(log_g, K)
    causal = (jnp.arange(l)[:, None] >= jnp.arange(l)[None, :])[None, :, :, None, None]

    def step(S, inp):
        qi, ki, vi, gi = inp
        G = jnp.cumsum(gi, axis=1)
        o_inter = jnp.einsum("blhk,bhkv->blhv", qi * jnp.exp(G), S)
        dec = jnp.exp(jnp.where(causal, G[:, :, None] - G[:, None, :], -jnp.inf))
        att = jnp.einsum("bthk,bshk,btshk->bhts", qi, ki, dec)
        o_intra = jnp.einsum("bhts,bshv->bthv", att, vi)
        g_last = G[:, -1]
        S_new = jnp.exp(g_last)[..., None] * S + jnp.einsum(
            "bshk,bshv->bhkv", ki * jnp.exp(g_last[:, None] - G), vi)
        return S_new, o_inter + o_intra

    S0 = jnp.zeros((b, H, K, V), jnp.float32)
    _, o = lax.scan(step, S0, (qc, kc, vc, gc))
    o = jnp.moveaxis(o, 0, 1).reshape(b, s, H, V)
    o = _rms_norm(o, norm_w)
    return o.reshape(b, s, GLA_WIDTH)


def setup_inputs(seed: int = 0) -> dict:
    key = jax.random.key(seed)
    ks = jax.random.split(key, 16)
    f32 = jnp.float32
    x = jax.random.normal(ks[0], (BATCH, SEQ, D_MODEL), f32)
    norm_w = 1.0 + 0.02 * jax.random.normal(ks[1], (DEPTH, D_MODEL), f32)
    w_in = jax.random.normal(ks[2], (DEPTH, D_MODEL, D_IN_PROJ), f32) * D_MODEL ** -0.5
    ssd_conv_w = jax.random.normal(ks[3], (DEPTH, SSD_CONV, SSD_CONV_DIM), f32) * SSD_CONV ** -0.5
    ssd_conv_b = 0.02 * jax.random.normal(ks[4], (DEPTH, SSD_CONV_DIM), f32)
    dt0 = jnp.exp(jax.random.uniform(ks[5], (DEPTH, SSD_HEADS), f32) * (math_log(0.1) - math_log(0.001)) + math_log(0.001))
    ssd_dt_bias = dt0 + jnp.log(-jnp.expm1(-dt0))
    ssd_a_log = jnp.log(jax.random.uniform(ks[6], (DEPTH, SSD_HEADS), f32, 1.0, 16.0))
    ssd_d = 1.0 + 0.02 * jax.random.normal(ks[7], (DEPTH, SSD_HEADS), f32)
    ssd_norm_w = 1.0 + 0.02 * jax.random.normal(ks[8], (DEPTH, SSD_WIDTH), f32)
    gla_gate_w = jax.random.normal(ks[9], (DEPTH, GLA_GATE_RANK, GLA_KEY), f32) * GLA_GATE_RANK ** -0.5
    gla_gate_b = 0.1 * jax.random.normal(ks[10], (DEPTH, GLA_KEY), f32)
    gla_norm_w = 1.0 + 0.02 * jax.random.normal(ks[11], (DEPTH, GLA_HEAD_V), f32)
    w_out = jax.random.normal(ks[12], (DEPTH, D_INNER, D_MODEL), f32) * D_INNER ** -0.5
    final_norm_w = 1.0 + 0.02 * jax.random.normal(ks[13], (D_MODEL,), f32)
    return {"x": x, "norm_w": norm_w, "w_in": w_in, "ssd_conv_w": ssd_conv_w,
            "ssd_conv_b": ssd_conv_b, "ssd_dt_bias": ssd_dt_bias, "ssd_a_log": ssd_a_log,
            "ssd_d": ssd_d, "ssd_norm_w": ssd_norm_w, "gla_gate_w": gla_gate_w,
            "gla_gate_b": gla_gate_b, "gla_norm_w": gla_norm_w, "w_out": w_out,
            "final_norm_w": final_norm_w}


def math_log(v):
    return float(np.log(v))


def reference(x, norm_w, w_in, ssd_conv_w, ssd_conv_b, ssd_dt_bias, ssd_a_log, ssd_d,
              ssd_norm_w, gla_gate_w, gla_gate_b, gla_norm_w, w_out, final_norm_w):
    b, s, _ = x.shape
    for layer in range(DEPTH):
        h = _rms_norm(x, norm_w[layer])
        proj = h @ w_in[layer].astype(jnp.float32)
        (sb_q, sb_k, sb_v, sb_z, ssd_z, ssd_xbc, ssd_dt,
         gla_q, gla_k, gla_v, gla_z, gla_glr) = _split_cols(proj, IN_SPLITS)
        y_a = _stick_breaking(sb_q.reshape(b, s, SB_HEADS, SB_HEAD_DIM),
                              sb_k.reshape(b, s, SB_HEADS, SB_HEAD_DIM),
                              sb_v.reshape(b, s, SB_HEADS, SB_HEAD_DIM)).reshape(b, s, SB_WIDTH)
        y_a = y_a * jax.nn.silu(sb_z)
        y_b = _ssd(ssd_xbc, ssd_dt, ssd_z, ssd_conv_w[layer], ssd_conv_b[layer],
                   ssd_dt_bias[layer], ssd_a_log[layer], ssd_d[layer], ssd_norm_w[layer])
        y_c = _gla(gla_q, gla_k, gla_v, gla_glr, gla_gate_w[layer], gla_gate_b[layer],
                   gla_norm_w[layer]) * jax.nn.silu(gla_z)
        y = jnp.concatenate([y_a, y_b, y_c], axis=-1) @ w_out[layer].astype(jnp.float32)
        x = (x.astype(jnp.float32) + y).astype(x.dtype)
    return _rms_norm(x, final_norm_w).astype(x.dtype)
```

```python
import functools

import jax
import jax.numpy as jnp
import numpy as np
from jax import lax
from jax.experimental import pallas as pl
from jax.experimental.pallas import tpu as pltpu

F32 = jnp.float32
BF16 = jnp.bfloat16

D_MODEL = 1024
DEPTH = 2
D_INNER = 2 * D_MODEL
SB_WIDTH = D_INNER // 4
SB_HEAD_DIM = 64
SSD_WIDTH = D_INNER // 2
SSD_HEAD_DIM = 64
SSD_HEADS = SSD_WIDTH // SSD_HEAD_DIM
SSD_GROUPS = 2
SSD_STATE = 128
SSD_CONV = 4
SSD_CONV_DIM = SSD_WIDTH + 2 * SSD_GROUPS * SSD_STATE
GLA_WIDTH = D_INNER // 4
GLA_HEADS = 4
GLA_KEY = GLA_WIDTH // 2
GLA_HEAD_K = GLA_KEY // GLA_HEADS
GLA_HEAD_V = GLA_WIDTH // GLA_HEADS
GLA_GATE_RANK = 16
GLA_GATE_TAU = 16.0
NORM_EPS = 1e-6

LANES = 128
SUBLANES = 8
CHUNK = 128
N_MAIN = 4 * SB_WIDTH + SSD_WIDTH + SSD_CONV_DIM + 2 * GLA_KEY + 2 * GLA_WIDTH
OFF_SSD_Z = 4 * SB_WIDTH
OFF_SSD_XBC = OFF_SSD_Z + SSD_WIDTH
OFF_GLA_Q = OFF_SSD_XBC + SSD_CONV_DIM
OFF_GLA_K = OFF_GLA_Q + GLA_KEY
OFF_GLA_V = OFF_GLA_K + GLA_KEY
OFF_GLA_Z = OFF_GLA_V + GLA_WIDTH
GLA_LEVELS = 7
VMEM_LIMIT = 56 * 1024 * 1024


def _dot(a, b):
    return jnp.dot(a, b, preferred_element_type=F32)


def _dot_nt(a, b):
    return lax.dot_general(a, b, (((1,), (1,)), ((), ())), preferred_element_type=F32)


def _dot_tn(a, b):
    return lax.dot_general(a, b, (((0,), (0,)), ((), ())), preferred_element_type=F32)


def _split2(x):
    hi = x.astype(BF16)
    lo = (x - hi.astype(F32)).astype(BF16)
    return hi, lo


def _split3(x):
    hi = x.astype(BF16)
    r = x - hi.astype(F32)
    mid = r.astype(BF16)
    lo = (r - mid.astype(F32)).astype(BF16)
    return hi, mid, lo


def _sigmoid(x):
    return 1.0 / (1.0 + jnp.exp(-x))


def _log_sigmoid(x):
    return jnp.minimum(x, 0.0) - jnp.log(1.0 + jnp.exp(-jnp.abs(x)))


def _softplus(x):
    return jnp.maximum(x, 0.0) + jnp.log(1.0 + jnp.exp(-jnp.abs(x)))


def _inproj_kernel(x_ref, nw_ref, w_ref, ws_ref, o_ref, os_ref, *, n_chunk):
    x = x_ref[...]
    ms = jnp.mean(x * x, axis=-1, keepdims=True)
    h = (x * lax.rsqrt(ms + NORM_EPS) * nw_ref[...]).astype(BF16)
    for c0 in range(0, N_MAIN, n_chunk):
        o_ref[:, c0:c0 + n_chunk] = _dot(h, w_ref[:, c0:c0 + n_chunk]).astype(BF16)
    os_ref[...] = _dot(h, ws_ref[...])


def _inproj(x2, nw, w_main, w_small, *, tm=512, n_chunk=1536):
    t = x2.shape[0]
    return pl.pallas_call(
        functools.partial(_inproj_kernel, n_chunk=n_chunk),
        grid=(t // tm,),
        in_specs=[
            pl.BlockSpec((tm, D_MODEL), lambda i: (i, 0)),
            pl.BlockSpec((1, D_MODEL), lambda i: (0, 0)),
            pl.BlockSpec((D_MODEL, N_MAIN), lambda i: (0, 0)),
            pl.BlockSpec((D_MODEL, LANES), lambda i: (0, 0)),
        ],
        out_specs=[
            pl.BlockSpec((tm, N_MAIN), lambda i: (i, 0)),
            pl.BlockSpec((tm, LANES), lambda i: (i, 0)),
        ],
        out_shape=[
            jax.ShapeDtypeStruct((t, N_MAIN), BF16),
            jax.ShapeDtypeStruct((t, LANES), F32),
        ],
        compiler_params=pltpu.CompilerParams(
            dimension_semantics=("arbitrary",), vmem_limit_bytes=VMEM_LIMIT),
        name="inproj",
    )(x2, nw, w_main, w_small)


def _sb_kernel(q_ref, k_ref, v_ref, z_ref, o_ref, k0_s, k1_s, vt_s, *, seq, tq):
    kb_n = LANES
    n_kb = seq // kb_n
    groups = kb_n // SUBLANES
    hd = SB_HEAD_DIM

    lane = lax.broadcasted_iota(jnp.int32, (kb_n, kb_n), 1)
    row = lax.broadcasted_iota(jnp.int32, (kb_n, kb_n), 0)
    src = (row % SUBLANES) * groups + row // SUBLANES
    perm = jnp.where(lane == src, 1.0, 0.0).astype(BF16)
    src_t = (lane % SUBLANES) * groups + lane // SUBLANES
    perm_t = jnp.where(row == src_t, 1.0, 0.0).astype(BF16)
    lo_lanes = lane < hd

    def prep(kb, carry):
        r0 = pl.multiple_of(kb * kb_n, kb_n)
        kp = _dot(perm, k_ref[pl.ds(r0, kb_n), :]).astype(BF16)
        k0_s[kb] = jnp.where(lo_lanes, kp, jnp.zeros_like(kp))
        k1_s[kb] = jnp.where(lo_lanes, jnp.zeros_like(kp), kp)
        vt = v_ref[pl.ds(r0, kb_n), :].astype(F32).T.astype(BF16)
        vt_s[kb] = _dot(vt, perm_t).astype(BF16)
        return carry

    lax.fori_loop(0, n_kb, prep, 0)

    seg = lax.broadcasted_iota(jnp.int32, (SUBLANES, tq), 0)
    qlane = lax.broadcasted_iota(jnp.int32, (SUBLANES, tq), 1)
    kpq = tq // kb_n

    def head_tile(kh, vth, qb, carry, acc, base):
        zt = _dot_nt(kh, qb)
        run = jnp.zeros((SUBLANES, tq), F32)
        s_rows = [None] * groups
        for i in reversed(range(groups)):
            z = zt[i * SUBLANES:(i + 1) * SUBLANES, :]
            lb = _log_sigmoid(z)
            lrest = lb - z
            if base is not None:
                lrest = jnp.where(i < base, lrest, 0.0)
            s_rows[i] = lb + run
            run = run + lrest
        tot = run
        inc = tot
        for sh in (1, 2, 4):
            inc = inc + jnp.where(seg < SUBLANES - sh, pltpu.roll(inc, SUBLANES - sh, 0), 0.0)
        off = (inc - tot) + carry
        carry = carry + jnp.broadcast_to(inc[0:1, :], (SUBLANES, tq))
        a_rows = []
        for i in range(groups):
            a = jnp.exp(s_rows[i] + off)
            if base is not None:
                a = jnp.where(i < base, a, 0.0)
            a_rows.append(a)
        at = jnp.concatenate(a_rows, axis=0).astype(BF16)
        acc = acc + _dot(vth, at)
        return carry, acc

    def qblock(qi, _):
        q0 = pl.multiple_of(qi * tq, tq)
        qb = (q_ref[pl.ds(q0, tq), :].astype(F32) * (hd ** -0.5)).astype(BF16)

        def step(kb, st, masked):
            c0, c1, a0, a1 = st
            base = None
            if masked:
                base = (q0 + qlane) - (kb * kb_n + seg * groups)
            vt = vt_s[kb]
            c0, a0 = head_tile(k0_s[kb], vt[0:hd, :], qb, c0, a0, base)
            c1, a1 = head_tile(k1_s[kb], vt[hd:2 * hd, :], qb, c1, a1, base)
            return c0, c1, a0, a1

        zc = jnp.zeros((SUBLANES, tq), F32)
        za = jnp.zeros((hd, tq), F32)
        st = (zc, zc, za, za)
        first = qi * kpq
        for j in reversed(range(kpq)):
            st = step(first + j, st, True)
        st = lax.fori_loop(0, first, lambda n, s: step(first - 1 - n, s, False), st)
        o = jnp.concatenate([st[2], st[3]], axis=0).T
        zg = z_ref[pl.ds(q0, tq), :].astype(F32)
        o_ref[pl.ds(q0, tq), :] = (o * (zg * _sigmoid(zg))).astype(BF16)
        return 0

    lax.fori_loop(0, seq // tq, qblock, 0)


def _sb_attention(proj, batch, seq, *, tq=128):
    pairs = SB_WIDTH // LANES
    n_kb = seq // LANES
    blk = lambda off: pl.BlockSpec((seq, LANES), lambda b, p, off=off: (b, off + p))
    return pl.pallas_call(
        functools.partial(_sb_kernel, seq=seq, tq=tq),
        grid=(batch, pairs),
        in_specs=[blk(0), blk(pairs), blk(2 * pairs), blk(3 * pairs)],
        out_specs=pl.BlockSpec((seq, LANES), lambda b, p: (b, p)),
        out_shape=jax.ShapeDtypeStruct((batch * seq, SB_WIDTH), BF16),
        scratch_shapes=[
            pltpu.VMEM((n_kb, LANES, LANES), BF16),
            pltpu.VMEM((n_kb, LANES, LANES), BF16),
            pltpu.VMEM((n_kb, LANES, LANES), BF16),
        ],
        compiler_params=pltpu.CompilerParams(
            dimension_semantics=("arbitrary", "arbitrary"), vmem_limit_bytes=VMEM_LIMIT),
        name="sb_attention",
    )(proj, proj, proj, proj)


def _ssd_kernel(z_ref, xbc_ref, sm_ref, cw_ref, cb_ref, dtb_ref, alog_ref, dexp_ref, nw_ref,
                e16_ref, tri_ref, o_ref, ext_s, h_s):
    L = CHUNK
    gw = SSD_WIDTH // SSD_GROUPS
    hpg = SSD_HEADS // SSD_GROUPS
    n = SSD_STATE

    @pl.when(pl.program_id(1) == 0)
    def _():
        ext_s[0:SUBLANES, :] = jnp.zeros((SUBLANES, SSD_CONV_DIM), F32)
        h_s[...] = jnp.zeros_like(h_s)

    ext_s[SUBLANES:SUBLANES + L, :] = xbc_ref[...].astype(F32)
    acc = cb_ref[...] + cw_ref[0:1, :] * ext_s[SUBLANES - 3:SUBLANES - 3 + L, :]
    for j in range(1, SSD_CONV):
        acc = acc + cw_ref[j:j + 1, :] * ext_s[SUBLANES - 3 + j:SUBLANES - 3 + j + L, :]
    ext_s[0:SUBLANES, :] = ext_s[L:L + SUBLANES, :]
    xbc = acc * _sigmoid(acc)
    xs = xbc[:, 0:SSD_WIDTH]

    lane = lax.broadcasted_iota(jnp.int32, (L, LANES), 1)
    head_lanes = lane < SSD_HEADS
    dt = jnp.where(head_lanes, _softplus(sm_ref[...] + dtb_ref[...]), 0.0)
    a = dt * (-jnp.exp(alog_ref[...]))
    tri = tri_ref[...]
    a1, a2, a3 = _split3(a)
    cs = _dot(tri, a1) + _dot(tri, a2) + _dot(tri, a3)
    cs_t = cs.T
    ecs = jnp.exp(cs)
    dend = jnp.exp(cs[L - 1:L, :] - cs)
    stack = jnp.concatenate([dt, dt * dend, ecs], axis=0)
    s_hi, s_lo = _split2(stack)
    e16 = e16_ref[...]
    ex = _dot(s_hi, e16) + _dot(s_lo, e16)
    x_dt = (xs * ex[0:L, :]).astype(BF16)
    x_st = (xs * ex[L:2 * L, :]).astype(BF16)
    ecs_x = ex[2 * L:3 * L, :]

    rowi = lax.broadcasted_iota(jnp.int32, (L, L), 0)
    coli = lax.broadcasted_iota(jnp.int32, (L, L), 1)
    causal = rowi >= coli
    lo_lanes = coli < SSD_HEAD_DIM

    y_parts = []
    for g in range(SSD_GROUPS):
        bm = xbc[:, SSD_WIDTH + g * n:SSD_WIDTH + (g + 1) * n].astype(BF16)
        cm = xbc[:, SSD_WIDTH + SSD_GROUPS * n + g * n:SSD_WIDTH + SSD_GROUPS * n + (g + 1) * n].astype(BF16)
        scores = _dot_nt(cm, bm)
        for pr in range(hpg // 2):
            col = g * gw + pr * LANES
            xp = x_dt[:, col:col + LANES]
            yd = None
            for half in range(2):
                h = g * hpg + 2 * pr + half
                sg = cs[:, h:h + 1] - cs_t[h:h + 1, :]
                dec = jnp.where(causal, jnp.exp(jnp.where(causal, sg, 0.0)), 0.0)
                w = (scores * dec).astype(BF16)
                xm = jnp.where(lo_lanes, xp, jnp.zeros_like(xp)) if half == 0 else jnp.where(
                    lo_lanes, jnp.zeros_like(xp), xp)
                t = _dot(w, xm)
                yd = t if yd is None else yd + t
            y_parts.append(yd)
        h_prev = h_s[g]
        y_off = _dot(cm, h_prev.astype(BF16)) * ecs_x[:, g * gw:(g + 1) * gw]
        y_parts.append(y_off)
        st_new = _dot_tn(bm, x_st[:, g * gw:(g + 1) * gw])
        h_s[g] = h_prev * ecs_x[L - 1:L, g * gw:(g + 1) * gw] + st_new

    per_g = 1 + hpg // 2
    outs = []
    zg = z_ref[...].astype(F32)
    for g in range(SSD_GROUPS):
        yd = jnp.concatenate(y_parts[g * per_g:g * per_g + hpg // 2], axis=1)
        y = yd + y_parts[g * per_g + hpg // 2] + dexp_ref[:, g * gw:(g + 1) * gw] * xs[:, g * gw:(g + 1) * gw]
        zz = zg[:, g * gw:(g + 1) * gw]
        y = y * (zz * _sigmoid(zz))
        ms = jnp.mean(y * y, axis=-1, keepdims=True)
        outs.append(y * lax.rsqrt(ms + NORM_EPS) * nw_ref[:, g * gw:(g + 1) * gw])
    o_ref[...] = jnp.concatenate(outs, axis=1).astype(BF16)


def _ssd(proj, small, cw, cb, dtb, alog, dexp, nw, e16, tri, batch, seq):
    nc = seq // CHUNK
    row = lambda b, c: b * nc + c
    const = lambda shape: pl.BlockSpec(shape, lambda b, c: (0,) * len(shape))
    return pl.pallas_call(
        _ssd_kernel,
        grid=(batch, nc),
        in_specs=[
            pl.BlockSpec((CHUNK, SSD_WIDTH), lambda b, c: (row(b, c), OFF_SSD_Z // SSD_WIDTH)),
            pl.BlockSpec((CHUNK, SSD_CONV_DIM), lambda b, c: (row(b, c), OFF_SSD_XBC // SSD_CONV_DIM)),
            pl.BlockSpec((CHUNK, LANES), lambda b, c: (row(b, c), 0)),
            const((SSD_CONV, SSD_CONV_DIM)), const((1, SSD_CONV_DIM)), const((1, LANES)), const((1, LANES)),
            const((1, SSD_WIDTH)), const((1, SSD_WIDTH)), const((LANES, SSD_WIDTH)), const((CHUNK, CHUNK)),
        ],
        out_specs=pl.BlockSpec((CHUNK, SSD_WIDTH), lambda b, c: (row(b, c), 0)),
        out_shape=jax.ShapeDtypeStruct((batch * seq, SSD_WIDTH), BF16),
        scratch_shapes=[
            pltpu.VMEM((CHUNK + SUBLANES, SSD_CONV_DIM), F32),
            pltpu.VMEM((SSD_GROUPS, SSD_STATE, SSD_WIDTH // SSD_GROUPS), F32),
        ],
        compiler_params=pltpu.CompilerParams(
            dimension_semantics=("arbitrary", "arbitrary"), vmem_limit_bytes=VMEM_LIMIT),
        name="ssd",
    )(proj, proj, small, cw, cb, dtb, alog, dexp, nw, e16, tri)


def _gla_kernel(q_ref, k_ref, v_ref, z_ref, sm_ref, gw_ref, gb_ref, nw_ref, tri_ref, sel_ref,
                msk_ref, o_ref, st_s):
    L = CHUNK
    kk = GLA_HEAD_K
    vv = GLA_HEAD_V

    @pl.when(pl.program_id(1) == 0)
    def _():
        st_s[...] = jnp.zeros_like(st_s)

    s_hi, s_lo = _split2(sm_ref[...])
    g_hi, g_lo = _split2(gw_ref[...])
    u = _dot(s_hi, g_hi) + _dot(s_lo, g_hi) + _dot(s_hi, g_lo) + gb_ref[...]
    lg = _log_sigmoid(u) * (1.0 / GLA_GATE_TAU)
    tri = tri_ref[...]
    l1, l2, l3 = _split3(lg)
    gc = _dot(tri, l1) + _dot(tri, l2) + _dot(tri, l3)
    c1, c2, c3 = _split3(gc)
    sel = sel_ref[...]
    ref_g = _dot(sel, c1) + _dot(sel, c2) + _dot(sel, c3)

    q = q_ref[...].astype(F32) * (kk ** -0.5)
    k = k_ref[...].astype(F32)
    lane = lax.broadcasted_iota(jnp.int32, (L, GLA_KEY), 1)
    rowi = lax.broadcasted_iota(jnp.int32, (L, GLA_KEY), 0)
    head_of_lane = lane // kk

    def head_mask(xb, h):
        return jnp.where(head_of_lane == h, xb, jnp.zeros_like(xb))

    att = [None] * GLA_HEADS
    for lv in range(GLA_LEVELS):
        expo = -jnp.abs(gc - ref_g[lv * L:(lv + 1) * L, :])
        upper = ((rowi >> lv) & 1) == 1
        xb = (jnp.where(upper, q, k) * jnp.exp(expo)).astype(BF16)
        m = msk_ref[lv]
        for h in range(GLA_HEADS):
            p = _dot_nt(head_mask(xb, h), xb) * m
            att[h] = p if att[h] is None else att[h] + p
    qb = q.astype(BF16)
    kb = k.astype(BF16)
    m = msk_ref[GLA_LEVELS]
    for h in range(GLA_HEADS):
        att[h] = att[h] + _dot_nt(head_mask(qb, h), kb) * m

    g_last = gc[L - 1:L, :]
    q_in = (q * jnp.exp(gc)).astype(BF16)
    k_dec = (k * jnp.exp(g_last - gc)).astype(BF16)
    st = st_s[...]
    st_b = st.astype(BF16)
    st_new = st * jnp.exp(g_last)
    lane_s = lax.broadcasted_iota(jnp.int32, (vv, GLA_KEY), 1) // kk
    outs = []
    for h in range(GLA_HEADS):
        vh = v_ref[:, h * vv:(h + 1) * vv]
        o = _dot_nt(head_mask(q_in, h), st_b) + _dot(att[h].astype(BF16), vh)
        upd = _dot_tn(vh, k_dec)
        st_new = st_new + jnp.where(lane_s == h, upd, 0.0)
        ms = jnp.mean(o * o, axis=-1, keepdims=True)
        o = o * lax.rsqrt(ms + NORM_EPS) * nw_ref[...]
        zz = z_ref[:, h * vv:(h + 1) * vv].astype(F32)
        outs.append(o * (zz * _sigmoid(zz)))
    st_s[...] = st_new
    o_ref[...] = jnp.concatenate(outs, axis=1).astype(BF16)


def _gla(proj, small, gw, gb, nw, tri, sel, msk, batch, seq):
    nc = seq // CHUNK
    row = lambda b, c: b * nc + c
    const = lambda shape: pl.BlockSpec(shape, lambda b, c: (0,) * len(shape))
    return pl.pallas_call(
        _gla_kernel,
        grid=(batch, nc),
        in_specs=[
            pl.BlockSpec((CHUNK, GLA_KEY), lambda b, c: (row(b, c), OFF_GLA_Q // GLA_KEY)),
            pl.BlockSpec((CHUNK, GLA_KEY), lambda b, c: (row(b, c), OFF_GLA_K // GLA_KEY)),
            pl.BlockSpec((CHUNK, GLA_WIDTH), lambda b, c: (row(b, c), OFF_GLA_V // GLA_WIDTH)),
            pl.BlockSpec((CHUNK, GLA_WIDTH), lambda b, c: (row(b, c), OFF_GLA_Z // GLA_WIDTH)),
            pl.BlockSpec((CHUNK, LANES), lambda b, c: (row(b, c), 0)),
            const((LANES, GLA_KEY)), const((1, GLA_KEY)), const((1, GLA_HEAD_V)),
            const((CHUNK, CHUNK)), const((GLA_LEVELS * CHUNK, CHUNK)), const((GLA_LEVELS + 1, CHUNK, CHUNK)),
        ],
        out_specs=pl.BlockSpec((CHUNK, GLA_WIDTH), lambda b, c: (row(b, c), 0)),
        out_shape=jax.ShapeDtypeStruct((batch * seq, GLA_WIDTH), BF16),
        scratch_shapes=[pltpu.VMEM((GLA_HEAD_V, GLA_KEY), F32)],
        compiler_params=pltpu.CompilerParams(
            dimension_semantics=("arbitrary", "arbitrary"), vmem_limit_bytes=VMEM_LIMIT),
        name="gla",
    )(proj, proj, proj, proj, small, gw, gb, nw, tri, sel, msk)


def _outproj_kernel(x_ref, ya_ref, yb_ref, yc_ref, w_ref, fw_ref, o_ref, *, final):
    y = _dot(ya_ref[...], w_ref[0:SB_WIDTH, :])
    y = y + _dot(yb_ref[...], w_ref[SB_WIDTH:SB_WIDTH + SSD_WIDTH, :])
    y = y + _dot(yc_ref[...], w_ref[SB_WIDTH + SSD_WIDTH:D_INNER, :])
    xn = x_ref[...] + y
    if final:
        ms = jnp.mean(xn * xn, axis=-1, keepdims=True)
        xn = xn * lax.rsqrt(ms + NORM_EPS) * fw_ref[...]
    o_ref[...] = xn


def _outproj(x2, ya, yb, yc, w, fw, *, final, tm=512):
    t = x2.shape[0]
    rows = lambda width: pl.BlockSpec((tm, width), lambda i: (i, 0))
    return pl.pallas_call(
        functools.partial(_outproj_kernel, final=final),
        grid=(t // tm,),
        in_specs=[rows(D_MODEL), rows(SB_WIDTH), rows(SSD_WIDTH), rows(GLA_WIDTH),
                  pl.BlockSpec((D_INNER, D_MODEL), lambda i: (0, 0)),
                  pl.BlockSpec((1, D_MODEL), lambda i: (0, 0))],
        out_specs=rows(D_MODEL),
        out_shape=jax.ShapeDtypeStruct((t, D_MODEL), F32),
        compiler_params=pltpu.CompilerParams(
            dimension_semantics=("arbitrary",), vmem_limit_bytes=VMEM_LIMIT),
        name="outproj_final" if final else "outproj",
    )(x2, ya, yb, yc, w, fw)


def _constants():
    L = CHUNK
    t = np.arange(L)
    tri = (t[:, None] >= t[None, :]).astype(np.float32)
    sel = np.zeros((GLA_LEVELS * L, L), np.float32)
    msk = np.zeros((GLA_LEVELS + 1, L, L), np.float32)
    for lv in range(GLA_LEVELS):
        mid = ((t >> (lv + 1)) << (lv + 1)) + (1 << lv)
        sel[lv * L + t, mid - 1] = 1.0
        same = (t[:, None] >> (lv + 1)) == (t[None, :] >> (lv + 1))
        up = ((t[:, None] >> lv) & 1) == 1
        low = ((t[None, :] >> lv) & 1) == 0
        msk[lv] = (same & up & low).astype(np.float32)
    msk[GLA_LEVELS] = np.eye(L, dtype=np.float32)
    e16 = np.zeros((LANES, SSD_WIDTH), np.float32)
    for h in range(SSD_HEADS):
        e16[h, h * SSD_HEAD_DIM:(h + 1) * SSD_HEAD_DIM] = 1.0
    return (jnp.asarray(tri, BF16), jnp.asarray(sel, BF16), jnp.asarray(msk, F32), jnp.asarray(e16, BF16))


def _pad_lanes(v, offset=0):
    out = jnp.zeros((1, LANES), F32)
    return out.at[0, offset:offset + v.shape[0]].set(v.astype(F32))


def kernel(x, norm_w, w_in, ssd_conv_w, ssd_conv_b, ssd_dt_bias, ssd_a_log, ssd_d, ssd_norm_w,
           gla_gate_w, gla_gate_b, gla_norm_w, w_out, final_norm_w):
    batch, seq, _ = x.shape
    assert seq % 256 == 0 and (batch * seq) % 512 == 0
    tri, sel, msk, e16 = _constants()
    x2 = x.reshape(batch * seq, D_MODEL).astype(F32)
    dt_col = OFF_GLA_Q
    glr_col = dt_col + SSD_HEADS + 2 * GLA_KEY + 2 * GLA_WIDTH
    for layer in range(DEPTH):
        w = w_in[layer]
        w_main = jnp.concatenate([w[:, :dt_col], w[:, dt_col + SSD_HEADS:glr_col]], axis=1).astype(BF16)
        w_small = jnp.zeros((D_MODEL, LANES), F32)
        w_small = w_small.at[:, 0:SSD_HEADS].set(w[:, dt_col:dt_col + SSD_HEADS])
        w_small = w_small.at[:, SSD_HEADS:SSD_HEADS + GLA_GATE_RANK].set(w[:, glr_col:glr_col + GLA_GATE_RANK])
        proj, small = _inproj(x2, norm_w[layer].reshape(1, D_MODEL), w_main, w_small.astype(BF16))

        ya = _sb_attention(proj, batch, seq)
        yb = _ssd(proj, small, ssd_conv_w[layer], ssd_conv_b[layer].reshape(1, SSD_CONV_DIM),
                  _pad_lanes(ssd_dt_bias[layer]), _pad_lanes(ssd_a_log[layer]),
                  jnp.repeat(ssd_d[layer], SSD_HEAD_DIM).reshape(1, SSD_WIDTH),
                  ssd_norm_w[layer].reshape(1, SSD_WIDTH), e16, tri, batch, seq)
        gw = jnp.zeros((LANES, GLA_KEY), F32).at[SSD_HEADS:SSD_HEADS + GLA_GATE_RANK, :].set(gla_gate_w[layer])
        yc = _gla(proj, small, gw, gla_gate_b[layer].reshape(1, GLA_KEY),
                  gla_norm_w[layer].reshape(1, GLA_HEAD_V), tri, sel, msk, batch, seq)
        x2 = _outproj(x2, ya, yb, yc, w_out[layer].astype(BF16), final_norm_w.reshape(1, D_MODEL),
                      final=(layer == DEPTH - 1))
    return x2.reshape(batch, seq, D_MODEL).astype(x.dtype)
```

```python
import functools

import jax
import jax.numpy as jnp
import numpy as np
from jax import lax
from jax.experimental import pallas as pl
from jax.experimental.pallas import tpu as pltpu

F32 = jnp.float32
BF16 = jnp.bfloat16

D_MODEL = 1024
DEPTH = 2
D_INNER = 2 * D_MODEL
SB_WIDTH = D_INNER // 4
SB_HEAD_DIM = 64
SSD_WIDTH = D_INNER // 2
SSD_HEAD_DIM = 64
SSD_HEADS = SSD_WIDTH // SSD_HEAD_DIM
SSD_GROUPS = 2
SSD_STATE = 128
SSD_CONV = 4
SSD_CONV_DIM = SSD_WIDTH + 2 * SSD_GROUPS * SSD_STATE
GLA_WIDTH = D_INNER // 4
GLA_HEADS = 4
GLA_KEY = GLA_WIDTH // 2
GLA_HEAD_K = GLA_KEY // GLA_HEADS
GLA_HEAD_V = GLA_WIDTH // GLA_HEADS
GLA_GATE_RANK = 16
GLA_GATE_TAU = 16.0
NORM_EPS = 1e-6

LANES = 128
SUBLANES = 8
CHUNK = 128
N_MAIN = 4 * SB_WIDTH + SSD_WIDTH + SSD_CONV_DIM + 2 * GLA_KEY + 2 * GLA_WIDTH
OFF_SSD_Z = 4 * SB_WIDTH
OFF_SSD_XBC = OFF_SSD_Z + SSD_WIDTH
OFF_GLA_Q = OFF_SSD_XBC + SSD_CONV_DIM
OFF_GLA_K = OFF_GLA_Q + GLA_KEY
OFF_GLA_V = OFF_GLA_K + GLA_KEY
OFF_GLA_Z = OFF_GLA_V + GLA_WIDTH
GLA_LEVELS = 7
SB_TQ = 512
LOG2E = 1.4426950408889634
VMEM_LIMIT = 56 * 1024 * 1024


def _dot(a, b):
    return jnp.dot(a, b, preferred_element_type=F32)


def _dot_nt(a, b):
    return lax.dot_general(a, b, (((1,), (1,)), ((), ())), preferred_element_type=F32)


def _dot_tn(a, b):
    return lax.dot_general(a, b, (((0,), (0,)), ((), ())), preferred_element_type=F32)


def _split2(x):
    hi = x.astype(BF16)
    lo = (x - hi.astype(F32)).astype(BF16)
    return hi, lo


def _split3(x):
    hi = x.astype(BF16)
    r = x - hi.astype(F32)
    mid = r.astype(BF16)
    lo = (r - mid.astype(F32)).astype(BF16)
    return hi, mid, lo


def _sigmoid(x):
    return 1.0 / (1.0 + jnp.exp(-x))


def _log_sigmoid(x):
    return jnp.minimum(x, 0.0) - jnp.log(1.0 + jnp.exp(-jnp.abs(x)))


def _softplus(x):
    return jnp.maximum(x, 0.0) + jnp.log(1.0 + jnp.exp(-jnp.abs(x)))


def _inproj_kernel(x_ref, nw_ref, w_ref, ws_ref, o_ref, os_ref, *, n_chunk):
    x = x_ref[...]
    ms = jnp.mean(x * x, axis=-1, keepdims=True)
    h = (x * lax.rsqrt(ms + NORM_EPS) * nw_ref[...]).astype(BF16)
    for c0 in range(0, N_MAIN, n_chunk):
        o_ref[:, c0:c0 + n_chunk] = _dot(h, w_ref[:, c0:c0 + n_chunk]).astype(BF16)
    os_ref[...] = _dot(h, ws_ref[...])


def _inproj(x2, nw, w_main, w_small, *, tm=512, n_chunk=1536):
    t = x2.shape[0]
    return pl.pallas_call(
        functools.partial(_inproj_kernel, n_chunk=n_chunk),
        grid=(t // tm,),
        in_specs=[
            pl.BlockSpec((tm, D_MODEL), lambda i: (i, 0)),
            pl.BlockSpec((1, D_MODEL), lambda i: (0, 0)),
            pl.BlockSpec((D_MODEL, N_MAIN), lambda i: (0, 0)),
            pl.BlockSpec((D_MODEL, LANES), lambda i: (0, 0)),
        ],
        out_specs=[
            pl.BlockSpec((tm, N_MAIN), lambda i: (i, 0)),
            pl.BlockSpec((tm, LANES), lambda i: (i, 0)),
        ],
        out_shape=[
            jax.ShapeDtypeStruct((t, N_MAIN), BF16),
            jax.ShapeDtypeStruct((t, LANES), F32),
        ],
        compiler_params=pltpu.CompilerParams(
            dimension_semantics=("arbitrary",), vmem_limit_bytes=VMEM_LIMIT),
        name="inproj",
    )(x2, nw, w_main, w_small)


def _sb_kernel(q_ref, k_ref, v_ref, z_ref, o_ref, k0_s, k1_s, vt_s, acc_s, *, seq):
    kb_n = LANES
    n_kb = seq // kb_n
    groups = kb_n // SUBLANES
    hd = SB_HEAD_DIM
    tq = SB_TQ
    U = tq // kb_n

    lane = lax.broadcasted_iota(jnp.int32, (kb_n, kb_n), 1)
    row = lax.broadcasted_iota(jnp.int32, (kb_n, kb_n), 0)
    src = (row % SUBLANES) * groups + row // SUBLANES
    perm = jnp.where(lane == src, 1.0, 0.0).astype(BF16)
    src_t = (lane % SUBLANES) * groups + lane // SUBLANES
    perm_t = jnp.where(row == src_t, 1.0, 0.0).astype(BF16)
    lo_lanes = lane < hd

    def prep(n, carry):
        for j in range(U):
            kb = n * U + j
            r0 = pl.multiple_of(kb * kb_n, kb_n)
            kp = _dot(perm, k_ref[pl.ds(r0, kb_n), :]).astype(BF16)
            k0_s[kb] = jnp.where(lo_lanes, kp, jnp.zeros_like(kp))
            k1_s[kb] = jnp.where(lo_lanes, jnp.zeros_like(kp), kp)
            vt = v_ref[pl.ds(r0, kb_n), :].astype(F32).T.astype(BF16)
            vt_s[kb] = _dot(vt, perm_t).astype(BF16)
        return carry

    lax.fori_loop(0, n_kb // U, prep, 0)

    def block_scan(zt, base):
        nl = zt.shape[1]
        seg = lax.broadcasted_iota(jnp.int32, (SUBLANES, nl), 0)
        run = jnp.ones((SUBLANES, nl), F32)
        s_rows = [None] * groups
        for i in reversed(range(groups)):
            zn = zt[i * SUBLANES:(i + 1) * SUBLANES, :]
            beta = 1.0 / (1.0 + jnp.exp2(zn))
            omb = 1.0 - beta
            if base is not None:
                vis = i < base
                beta = jnp.where(vis, beta, 0.0)
                omb = jnp.where(vis, omb, 1.0)
            s_rows[i] = beta * run
            run = run * omb
        inc = run
        for sh in (1, 2, 4):
            inc = inc * jnp.where(seg < SUBLANES - sh, pltpu.roll(inc, SUBLANES - sh, 0), 1.0)
        excl = jnp.where(seg < SUBLANES - 1, pltpu.roll(inc, SUBLANES - 1, 0), 1.0)
        total = jnp.broadcast_to(inc[0:1, :], (SUBLANES, nl))
        return s_rows, excl, total

    def chunk(ztc, base, cc):
        s_rows, excl, total = block_scan(ztc, base)
        off = excl * cc
        a = jnp.concatenate([s_rows[i] * off for i in range(groups)], axis=0)
        return a, cc * total

    seg_d = lax.broadcasted_iota(jnp.int32, (SUBLANES, kb_n), 0)
    lane_d = lax.broadcasted_iota(jnp.int32, (SUBLANES, kb_n), 1)
    base_d = lane_d - seg_d * groups

    def super_tile(kb0, qb, carries, diag):
        k_cat = jnp.concatenate([k0_s[kb0 + j] for j in range(U)] + [k1_s[kb0 + j] for j in range(U)], axis=0)
        zt = _dot_nt(k_cat, qb)
        vts = [vt_s[kb0 + j] for j in range(U)]
        out = []
        for h in range(2):
            carry = carries[h]
            a_blocks = [None] * U
            for j in reversed(range(U)):
                r0 = (h * U + j) * kb_n
                if not diag:
                    a, carry = chunk(zt[r0:r0 + kb_n, :], None, carry)
                else:
                    l0 = j * kb_n
                    parts, cparts = [], []
                    if l0 > 0:
                        parts.append(jnp.zeros((kb_n, l0), F32))
                        cparts.append(carry[:, 0:l0])
                    a_m, c_m = chunk(zt[r0:r0 + kb_n, l0:l0 + kb_n], base_d, carry[:, l0:l0 + kb_n])
                    parts.append(a_m)
                    cparts.append(c_m)
                    if l0 + kb_n < tq:
                        a_u, c_u = chunk(zt[r0:r0 + kb_n, l0 + kb_n:tq], None, carry[:, l0 + kb_n:tq])
                        parts.append(a_u)
                        cparts.append(c_u)
                    a = jnp.concatenate(parts, axis=1)
                    carry = jnp.concatenate(cparts, axis=1)
                a_blocks[j] = a.astype(BF16)
            a_cat = jnp.concatenate(a_blocks, axis=0)
            v_cat = jnp.concatenate([vts[j][h * hd:(h + 1) * hd, :] for j in range(U)], axis=1)
            acc_s[h] += _dot(v_cat, a_cat)
            out.append(carry)
        return tuple(out)

    def qblock(qi, _):
        q0 = pl.multiple_of(qi * tq, tq)
        qb = (q_ref[pl.ds(q0, tq), :].astype(F32) * (-(hd ** -0.5) * LOG2E)).astype(BF16)
        acc_s[...] = jnp.zeros_like(acc_s)
        one = jnp.ones((SUBLANES, tq), F32)
        carries = super_tile(qi * U, qb, (one, one), True)
        lax.fori_loop(0, qi, lambda n, c: super_tile((qi - 1 - n) * U, qb, c, False), carries)
        o = jnp.concatenate([acc_s[0], acc_s[1]], axis=0).T
        zg = z_ref[pl.ds(q0, tq), :].astype(F32)
        o_ref[pl.ds(q0, tq), :] = (o * (zg * _sigmoid(zg))).astype(BF16)
        return 0

    lax.fori_loop(0, seq // tq, qblock, 0)


def _sb_attention(proj, batch, seq):
    pairs = SB_WIDTH // LANES
    n_kb = seq // LANES
    blk = lambda off: pl.BlockSpec((seq, LANES), lambda b, p, off=off: (b, off + p))
    return pl.pallas_call(
        functools.partial(_sb_kernel, seq=seq),
        grid=(batch, pairs),
        in_specs=[blk(0), blk(pairs), blk(2 * pairs), blk(3 * pairs)],
        out_specs=pl.BlockSpec((seq, LANES), lambda b, p: (b, p)),
        out_shape=jax.ShapeDtypeStruct((batch * seq, SB_WIDTH), BF16),
        scratch_shapes=[
            pltpu.VMEM((n_kb, LANES, LANES), BF16),
            pltpu.VMEM((n_kb, LANES, LANES), BF16),
            pltpu.VMEM((n_kb, LANES, LANES), BF16),
            pltpu.VMEM((2, SB_HEAD_DIM, SB_TQ), F32),
        ],
        compiler_params=pltpu.CompilerParams(
            dimension_semantics=("arbitrary", "arbitrary"), vmem_limit_bytes=VMEM_LIMIT),
        name="sb_attention",
    )(proj, proj, proj, proj)


def _ssd_kernel(z_ref, xbc_ref, sm_ref, cw_ref, cb_ref, dtb_ref, alog_ref, dexp_ref, nw_ref,
                e16_ref, tri_ref, o_ref, ext_s, h_s):
    L = CHUNK
    gw = SSD_WIDTH // SSD_GROUPS
    hpg = SSD_HEADS // SSD_GROUPS
    n = SSD_STATE

    @pl.when(pl.program_id(1) == 0)
    def _():
        ext_s[0:SUBLANES, :] = jnp.zeros((SUBLANES, SSD_CONV_DIM), F32)
        h_s[...] = jnp.zeros_like(h_s)

    ext_s[SUBLANES:SUBLANES + L, :] = xbc_ref[...].astype(F32)
    acc = cb_ref[...] + cw_ref[0:1, :] * ext_s[SUBLANES - 3:SUBLANES - 3 + L, :]
    for j in range(1, SSD_CONV):
        acc = acc + cw_ref[j:j + 1, :] * ext_s[SUBLANES - 3 + j:SUBLANES - 3 + j + L, :]
    ext_s[0:SUBLANES, :] = ext_s[L:L + SUBLANES, :]
    xbc = acc * _sigmoid(acc)
    xs = xbc[:, 0:SSD_WIDTH]

    lane = lax.broadcasted_iota(jnp.int32, (L, LANES), 1)
    head_lanes = lane < SSD_HEADS
    dt = jnp.where(head_lanes, _softplus(sm_ref[...] + dtb_ref[...]), 0.0)
    a = dt * (-jnp.exp(alog_ref[...]))
    tri = tri_ref[...]
    a1, a2, a3 = _split3(a)
    cs = _dot(tri, a1) + _dot(tri, a2) + _dot(tri, a3)
    cs_t = cs.T
    ecs = jnp.exp(cs)
    dend = jnp.exp(cs[L - 1:L, :] - cs)
    stack = jnp.concatenate([dt, dt * dend, ecs], axis=0)
    s_hi, s_lo = _split2(stack)
    e16 = e16_ref[...]
    ex = _dot(s_hi, e16) + _dot(s_lo, e16)
    x_dt = (xs * ex[0:L, :]).astype(BF16)
    x_st = (xs * ex[L:2 * L, :]).astype(BF16)
    ecs_x = ex[2 * L:3 * L, :]

    rowi = lax.broadcasted_iota(jnp.int32, (L, L), 0)
    coli = lax.broadcasted_iota(jnp.int32, (L, L), 1)
    causal = rowi >= coli
    lo_lanes = coli < SSD_HEAD_DIM

    y_parts = []
    for g in range(SSD_GROUPS):
        bm = xbc[:, SSD_WIDTH + g * n:SSD_WIDTH + (g + 1) * n].astype(BF16)
        cm = xbc[:, SSD_WIDTH + SSD_GROUPS * n + g * n:SSD_WIDTH + SSD_GROUPS * n + (g + 1) * n].astype(BF16)
        scores = _dot_nt(cm, bm)
        for pr in range(hpg // 2):
            col = g * gw + pr * LANES
            xp = x_dt[:, col:col + LANES]
            yd = None
            for half in range(2):
                h = g * hpg + 2 * pr + half
                sg = cs[:, h:h + 1] - cs_t[h:h + 1, :]
                dec = jnp.where(causal, jnp.exp(jnp.where(causal, sg, 0.0)), 0.0)
                w = (scores * dec).astype(BF16)
                xm = jnp.where(lo_lanes, xp, jnp.zeros_like(xp)) if half == 0 else jnp.where(
                    lo_lanes, jnp.zeros_like(xp), xp)
                t = _dot(w, xm)
                yd = t if yd is None else yd + t
            y_parts.append(yd)
        h_prev = h_s[g]
        y_off = _dot(cm, h_prev.astype(BF16)) * ecs_x[:, g * gw:(g + 1) * gw]
        y_parts.append(y_off)
        st_new = _dot_tn(bm, x_st[:, g * gw:(g + 1) * gw])
        h_s[g] = h_prev * ecs_x[L - 1:L, g * gw:(g + 1) * gw] + st_new

    per_g = 1 + hpg // 2
    outs = []
    zg = z_ref[...].astype(F32)
    for g in range(SSD_GROUPS):
        yd = jnp.concatenate(y_parts[g * per_g:g * per_g + hpg // 2], axis=1)
        y = yd + y_parts[g * per_g + hpg // 2] + dexp_ref[:, g * gw:(g + 1) * gw] * xs[:, g * gw:(g + 1) * gw]
        zz = zg[:, g * gw:(g + 1) * gw]
        y = y * (zz * _sigmoid(zz))
        ms = jnp.mean(y * y, axis=-1, keepdims=True)
        outs.append(y * lax.rsqrt(ms + NORM_EPS) * nw_ref[:, g * gw:(g + 1) * gw])
    o_ref[...] = jnp.concatenate(outs, axis=1).astype(BF16)


def _ssd(proj, small, cw, cb, dtb, alog, dexp, nw, e16, tri, batch, seq):
    nc = seq // CHUNK
    row = lambda b, c: b * nc + c
    const = lambda shape: pl.BlockSpec(shape, lambda b, c: (0,) * len(shape))
    return pl.pallas_call(
        _ssd_kernel,
        grid=(batch, nc),
        in_specs=[
            pl.BlockSpec((CHUNK, SSD_WIDTH), lambda b, c: (row(b, c), OFF_SSD_Z // SSD_WIDTH)),
            pl.BlockSpec((CHUNK, SSD_CONV_DIM), lambda b, c: (row(b, c), OFF_SSD_XBC // SSD_CONV_DIM)),
            pl.BlockSpec((CHUNK, LANES), lambda b, c: (row(b, c), 0)),
            const((SSD_CONV, SSD_CONV_DIM)), const((1, SSD_CONV_DIM)), const((1, LANES)), const((1, LANES)),
            const((1, SSD_WIDTH)), const((1, SSD_WIDTH)), const((LANES, SSD_WIDTH)), const((CHUNK, CHUNK)),
        ],
        out_specs=pl.BlockSpec((CHUNK, SSD_WIDTH), lambda b, c: (row(b, c), 0)),
        out_shape=jax.ShapeDtypeStruct((batch * seq, SSD_WIDTH), BF16),
        scratch_shapes=[
            pltpu.VMEM((CHUNK + SUBLANES, SSD_CONV_DIM), F32),
            pltpu.VMEM((SSD_GROUPS, SSD_STATE, SSD_WIDTH // SSD_GROUPS), F32),
        ],
        compiler_params=pltpu.CompilerParams(
            dimension_semantics=("arbitrary", "arbitrary"), vmem_limit_bytes=VMEM_LIMIT),
        name="ssd",
    )(proj, proj, small, cw, cb, dtb, alog, dexp, nw, e16, tri)


def _gla_kernel(q_ref, k_ref, v_ref, z_ref, sm_ref, gw_ref, gb_ref, nw_ref, tri_ref, sel_ref,
                msk_ref, o_ref, st_s):
    L = CHUNK
    kk = GLA_HEAD_K
    vv = GLA_HEAD_V

    @pl.when(pl.program_id(1) == 0)
    def _():
        st_s[...] = jnp.zeros_like(st_s)

    s_hi, s_lo = _split2(sm_ref[...])
    g_hi, g_lo = _split2(gw_ref[...])
    u = _dot(s_hi, g_hi) + _dot(s_lo, g_hi) + _dot(s_hi, g_lo) + gb_ref[...]
    lg = _log_sigmoid(u) * (1.0 / GLA_GATE_TAU)
    tri = tri_ref[...]
    l1, l2, l3 = _split3(lg)
    gc = _dot(tri, l1) + _dot(tri, l2) + _dot(tri, l3)
    c1, c2, c3 = _split3(gc)
    sel = sel_ref[...]
    ref_g = _dot(sel, c1) + _dot(sel, c2) + _dot(sel, c3)

    q = q_ref[...].astype(F32) * (kk ** -0.5)
    k = k_ref[...].astype(F32)
    lane = lax.broadcasted_iota(jnp.int32, (L, GLA_KEY), 1)
    rowi = lax.broadcasted_iota(jnp.int32, (L, GLA_KEY), 0)
    head_of_lane = lane // kk

    def head_mask(xb, h):
        return jnp.where(head_of_lane == h, xb, jnp.zeros_like(xb))

    att = [None] * GLA_HEADS
    for lv in range(GLA_LEVELS):
        expo = -jnp.abs(gc - ref_g[lv * L:(lv + 1) * L, :])
        upper = ((rowi >> lv) & 1) == 1
        xb = (jnp.where(upper, q, k) * jnp.exp(expo)).astype(BF16)
        m = msk_ref[lv]
        for h in range(GLA_HEADS):
            p = _dot_nt(head_mask(xb, h), xb) * m
            att[h] = p if att[h] is None else att[h] + p
    qb = q.astype(BF16)
    kb = k.astype(BF16)
    m = msk_ref[GLA_LEVELS]
    for h in range(GLA_HEADS):
        att[h] = att[h] + _dot_nt(head_mask(qb, h), kb) * m

    g_last = gc[L - 1:L, :]
    q_in = (q * jnp.exp(gc)).astype(BF16)
    k_dec = (k * jnp.exp(g_last - gc)).astype(BF16)
    st = st_s[...]
    st_b = st.astype(BF16)
    st_new = st * jnp.exp(g_last)
    lane_s = lax.broadcasted_iota(jnp.int32, (vv, GLA_KEY), 1) // kk
    outs = []
    for h in range(GLA_HEADS):
        vh = v_ref[:, h * vv:(h + 1) * vv]
        o = _dot_nt(head_mask(q_in, h), st_b) + _dot(att[h].astype(BF16), vh)
        upd = _dot_tn(vh, k_dec)
        st_new = st_new + jnp.where(lane_s == h, upd, 0.0)
        ms = jnp.mean(o * o, axis=-1, keepdims=True)
        o = o * lax.rsqrt(ms + NORM_EPS) * nw_ref[...]
        zz = z_ref[:, h * vv:(h + 1) * vv].astype(F32)
        outs.append(o * (zz * _sigmoid(zz)))
    st_s[...] = st_new
    o_ref[...] = jnp.concatenate(outs, axis=1).astype(BF16)


def _gla(proj, small, gw, gb, nw, tri, sel, msk, batch, seq):
    nc = seq // CHUNK
    row = lambda b, c: b * nc + c
    const = lambda shape: pl.BlockSpec(shape, lambda b, c: (0,) * len(shape))
    return pl.pallas_call(
        _gla_kernel,
        grid=(batch, nc),
        in_specs=[
            pl.BlockSpec((CHUNK, GLA_KEY), lambda b, c: (row(b, c), OFF_GLA_Q // GLA_KEY)),
            pl.BlockSpec((CHUNK, GLA_KEY), lambda b, c: (row(b, c), OFF_GLA_K // GLA_KEY)),
            pl.BlockSpec((CHUNK, GLA_WIDTH), lambda b, c: (row(b, c), OFF_GLA_V // GLA_WIDTH)),
            pl.BlockSpec((CHUNK, GLA_WIDTH), lambda b, c: (row(b, c), OFF_GLA_Z // GLA_WIDTH)),
            pl.BlockSpec((CHUNK, LANES), lambda b, c: (row(b, c), 0)),
            const((LANES, GLA_KEY)), const((1, GLA_KEY)), const((1, GLA_HEAD_V)),
            const((CHUNK, CHUNK)), const((GLA_LEVELS * CHUNK, CHUNK)), const((GLA_LEVELS + 1, CHUNK, CHUNK)),
        ],
        out_specs=pl.BlockSpec((CHUNK, GLA_WIDTH), lambda b, c: (row(b, c), 0)),
        out_shape=jax.ShapeDtypeStruct((batch * seq, GLA_WIDTH), BF16),
        scratch_shapes=[pltpu.VMEM((GLA_HEAD_V, GLA_KEY), F32)],
        compiler_params=pltpu.CompilerParams(
            dimension_semantics=("arbitrary", "arbitrary"), vmem_limit_bytes=VMEM_LIMIT),
        name="gla",
    )(proj, proj, proj, proj, small, gw, gb, nw, tri, sel, msk)


def _outproj_kernel(x_ref, ya_ref, yb_ref, yc_ref, w_ref, fw_ref, o_ref, *, final):
    y = _dot(ya_ref[...], w_ref[0:SB_WIDTH, :])
    y = y + _dot(yb_ref[...], w_ref[SB_WIDTH:SB_WIDTH + SSD_WIDTH, :])
    y = y + _dot(yc_ref[...], w_ref[SB_WIDTH + SSD_WIDTH:D_INNER, :])
    xn = x_ref[...] + y
    if final:
        ms = jnp.mean(xn * xn, axis=-1, keepdims=True)
        xn = xn * lax.rsqrt(ms + NORM_EPS) * fw_ref[...]
    o_ref[...] = xn


def _outproj(x2, ya, yb, yc, w, fw, *, final, tm=512):
    t = x2.shape[0]
    rows = lambda width: pl.BlockSpec((tm, width), lambda i: (i, 0))
    return pl.pallas_call(
        functools.partial(_outproj_kernel, final=final),
        grid=(t // tm,),
        in_specs=[rows(D_MODEL), rows(SB_WIDTH), rows(SSD_WIDTH), rows(GLA_WIDTH),
                  pl.BlockSpec((D_INNER, D_MODEL), lambda i: (0, 0)),
                  pl.BlockSpec((1, D_MODEL), lambda i: (0, 0))],
        out_specs=rows(D_MODEL),
        out_shape=jax.ShapeDtypeStruct((t, D_MODEL), F32),
        compiler_params=pltpu.CompilerParams(
            dimension_semantics=("arbitrary",), vmem_limit_bytes=VMEM_LIMIT),
        name="outproj_final" if final else "outproj",
    )(x2, ya, yb, yc, w, fw)


def _constants():
    L = CHUNK
    t = np.arange(L)
    tri = (t[:, None] >= t[None, :]).astype(np.float32)
    sel = np.zeros((GLA_LEVELS * L, L), np.float32)
    msk = np.zeros((GLA_LEVELS + 1, L, L), np.float32)
    for lv in range(GLA_LEVELS):
        mid = ((t >> (lv + 1)) << (lv + 1)) + (1 << lv)
        sel[lv * L + t, mid - 1] = 1.0
        same = (t[:, None] >> (lv + 1)) == (t[None, :] >> (lv + 1))
        up = ((t[:, None] >> lv) & 1) == 1
        low = ((t[None, :] >> lv) & 1) == 0
        msk[lv] = (same & up & low).astype(np.float32)
    msk[GLA_LEVELS] = np.eye(L, dtype=np.float32)
    e16 = np.zeros((LANES, SSD_WIDTH), np.float32)
    for h in range(SSD_HEADS):
        e16[h, h * SSD_HEAD_DIM:(h + 1) * SSD_HEAD_DIM] = 1.0
    return (jnp.asarray(tri, BF16), jnp.asarray(sel, BF16), jnp.asarray(msk, F32), jnp.asarray(e16, BF16))


def _pad_lanes(v, offset=0):
    out = jnp.zeros((1, LANES), F32)
    return out.at[0, offset:offset + v.shape[0]].set(v.astype(F32))


def kernel(x, norm_w, w_in, ssd_conv_w, ssd_conv_b, ssd_dt_bias, ssd_a_log, ssd_d, ssd_norm_w,
           gla_gate_w, gla_gate_b, gla_norm_w, w_out, final_norm_w):
    batch, seq, _ = x.shape
    assert seq % 256 == 0 and (batch * seq) % 512 == 0
    tri, sel, msk, e16 = _constants()
    x2 = x.reshape(batch * seq, D_MODEL).astype(F32)
    dt_col = OFF_GLA_Q
    glr_col = dt_col + SSD_HEADS + 2 * GLA_KEY + 2 * GLA_WIDTH
    for layer in range(DEPTH):
        w = w_in[layer]
        w_main = jnp.concatenate([w[:, :dt_col], w[:, dt_col + SSD_HEADS:glr_col]], axis=1).astype(BF16)
        w_small = jnp.zeros((D_MODEL, LANES), F32)
        w_small = w_small.at[:, 0:SSD_HEADS].set(w[:, dt_col:dt_col + SSD_HEADS])
        w_small = w_small.at[:, SSD_HEADS:SSD_HEADS + GLA_GATE_RANK].set(w[:, glr_col:glr_col + GLA_GATE_RANK])
        proj, small = _inproj(x2, norm_w[layer].reshape(1, D_MODEL), w_main, w_small.astype(BF16))

        ya = _sb_attention(proj, batch, seq)
        yb = _ssd(proj, small, ssd_conv_w[layer], ssd_conv_b[layer].reshape(1, SSD_CONV_DIM),
                  _pad_lanes(ssd_dt_bias[layer]), _pad_lanes(ssd_a_log[layer]),
                  jnp.repeat(ssd_d[layer], SSD_HEAD_DIM).reshape(1, SSD_WIDTH),
                  ssd_norm_w[layer].reshape(1, SSD_WIDTH), e16, tri, batch, seq)
        gw = jnp.zeros((LANES, GLA_KEY), F32).at[SSD_HEADS:SSD_HEADS + GLA_GATE_RANK, :].set(gla_gate_w[layer])
        yc = _gla(proj, small, gw, gla_gate_b[layer].reshape(1, GLA_KEY),
                  gla_norm_w[layer].reshape(1, GLA_HEAD_V), tri, sel, msk, batch, seq)
        x2 = _outproj(x2, ya, yb, yc, w_out[layer].astype(BF16), final_norm_w.reshape(1, D_MODEL),
                      final=(layer == DEPTH - 1))
    return x2.reshape(batch, seq, D_MODEL).astype(x.dtype)
```

```python
import functools

import jax
import jax.numpy as jnp
import numpy as np
from jax import lax
from jax.experimental import pallas as pl
from jax.experimental.pallas import tpu as pltpu

F32 = jnp.float32
BF16 = jnp.bfloat16

D_MODEL = 1024
DEPTH = 2
D_INNER = 2 * D_MODEL
SB_WIDTH = D_INNER // 4
SB_HEAD_DIM = 64
SSD_WIDTH = D_INNER // 2
SSD_HEAD_DIM = 64
SSD_HEADS = SSD_WIDTH // SSD_HEAD_DIM
SSD_GROUPS = 2
SSD_STATE = 128
SSD_CONV = 4
SSD_CONV_DIM = SSD_WIDTH + 2 * SSD_GROUPS * SSD_STATE
GLA_WIDTH = D_INNER // 4
GLA_HEADS = 4
GLA_KEY = GLA_WIDTH // 2
GLA_HEAD_K = GLA_KEY // GLA_HEADS
GLA_HEAD_V = GLA_WIDTH // GLA_HEADS
GLA_GATE_RANK = 16
GLA_GATE_TAU = 16.0
NORM_EPS = 1e-6

LANES = 128
SUBLANES = 8
CHUNK = 128
N_MAIN = 4 * SB_WIDTH + SSD_WIDTH + SSD_CONV_DIM + 2 * GLA_KEY + 2 * GLA_WIDTH
OFF_SSD_Z = 4 * SB_WIDTH
OFF_SSD_XBC = OFF_SSD_Z + SSD_WIDTH
OFF_GLA_Q = OFF_SSD_XBC + SSD_CONV_DIM
OFF_GLA_K = OFF_GLA_Q + GLA_KEY
OFF_GLA_V = OFF_GLA_K + GLA_KEY
OFF_GLA_Z = OFF_GLA_V + GLA_WIDTH
GLA_LEVELS = 7
GLA_DIRECT_MAX = 1e30
SB_TQ = 512
LOG2E = 1.4426950408889634
VMEM_LIMIT = 56 * 1024 * 1024


def _dot(a, b):
    return jnp.dot(a, b, preferred_element_type=F32)


def _dot_nt(a, b):
    return lax.dot_general(a, b, (((1,), (1,)), ((), ())), preferred_element_type=F32)


def _dot_tn(a, b):
    return lax.dot_general(a, b, (((0,), (0,)), ((), ())), preferred_element_type=F32)


def _split2(x):
    hi = x.astype(BF16)
    lo = (x - hi.astype(F32)).astype(BF16)
    return hi, lo


def _split3(x):
    hi = x.astype(BF16)
    r = x - hi.astype(F32)
    mid = r.astype(BF16)
    lo = (r - mid.astype(F32)).astype(BF16)
    return hi, mid, lo


def _sigmoid(x):
    return 1.0 / (1.0 + jnp.exp2(x * (-LOG2E)))


def _log_sigmoid(x):
    return jnp.minimum(x, 0.0) - jnp.log(1.0 + jnp.exp(-jnp.abs(x)))


def _softplus(x):
    return jnp.maximum(x, 0.0) + jnp.log(1.0 + jnp.exp(-jnp.abs(x)))


def _inproj_kernel(x_ref, nw_ref, w_ref, ws_ref, o_ref, os_ref, *, n_chunk):
    x = x_ref[...]
    ms = jnp.mean(x * x, axis=-1, keepdims=True)
    h = (x * lax.rsqrt(ms + NORM_EPS) * nw_ref[...]).astype(BF16)
    for c0 in range(0, N_MAIN, n_chunk):
        o_ref[:, c0:c0 + n_chunk] = _dot(h, w_ref[:, c0:c0 + n_chunk]).astype(BF16)
    os_ref[...] = _dot(h, ws_ref[...])


def _inproj(x2, nw, w_main, w_small, *, tm=512, n_chunk=1536):
    t = x2.shape[0]
    return pl.pallas_call(
        functools.partial(_inproj_kernel, n_chunk=n_chunk),
        grid=(t // tm,),
        in_specs=[
            pl.BlockSpec((tm, D_MODEL), lambda i: (i, 0)),
            pl.BlockSpec((1, D_MODEL), lambda i: (0, 0)),
            pl.BlockSpec((D_MODEL, N_MAIN), lambda i: (0, 0)),
            pl.BlockSpec((D_MODEL, LANES), lambda i: (0, 0)),
        ],
        out_specs=[
            pl.BlockSpec((tm, N_MAIN), lambda i: (i, 0)),
            pl.BlockSpec((tm, LANES), lambda i: (i, 0)),
        ],
        out_shape=[
            jax.ShapeDtypeStruct((t, N_MAIN), BF16),
            jax.ShapeDtypeStruct((t, LANES), F32),
        ],
        compiler_params=pltpu.CompilerParams(
            dimension_semantics=("arbitrary",), vmem_limit_bytes=VMEM_LIMIT),
        name="inproj",
    )(x2, nw, w_main, w_small)


def _sb_kernel(q_ref, k_ref, v_ref, z_ref, o_ref, k0_s, k1_s, vt_s, acc_s, *, seq):
    kb_n = LANES
    n_kb = seq // kb_n
    groups = kb_n // SUBLANES
    hd = SB_HEAD_DIM
    tq = SB_TQ
    U = tq // kb_n

    lane = lax.broadcasted_iota(jnp.int32, (kb_n, kb_n), 1)
    row = lax.broadcasted_iota(jnp.int32, (kb_n, kb_n), 0)
    src = (row % SUBLANES) * groups + row // SUBLANES
    perm = jnp.where(lane == src, 1.0, 0.0).astype(BF16)
    src_t = (lane % SUBLANES) * groups + lane // SUBLANES
    perm_t = jnp.where(row == src_t, 1.0, 0.0).astype(BF16)
    lo_lanes = lane < hd

    def prep(n, carry):
        for j in range(U):
            kb = n * U + j
            r0 = pl.multiple_of(kb * kb_n, kb_n)
            kp = _dot(perm, k_ref[pl.ds(r0, kb_n), :]).astype(BF16)
            k0_s[kb] = jnp.where(lo_lanes, kp, jnp.zeros_like(kp))
            k1_s[kb] = jnp.where(lo_lanes, jnp.zeros_like(kp), kp)
            vt = v_ref[pl.ds(r0, kb_n), :].astype(F32).T.astype(BF16)
            vt_s[kb] = _dot(vt, perm_t).astype(BF16)
        return carry

    lax.fori_loop(0, n_kb // U, prep, 0)

    def block_scan(zt, base):
        nl = zt.shape[1]
        seg = lax.broadcasted_iota(jnp.int32, (SUBLANES, nl), 0)
        run = jnp.ones((SUBLANES, nl), F32)
        s_rows = [None] * groups
        for i in reversed(range(groups)):
            zn = zt[i * SUBLANES:(i + 1) * SUBLANES, :]
            beta = 1.0 / (1.0 + jnp.exp2(zn))
            omb = 1.0 - beta
            if base is not None:
                vis = i < base
                beta = jnp.where(vis, beta, 0.0)
                omb = jnp.where(vis, omb, 1.0)
            s_rows[i] = beta * run
            run = run * omb
        inc = run
        for sh in (1, 2, 4):
            inc = inc * jnp.where(seg < SUBLANES - sh, pltpu.roll(inc, SUBLANES - sh, 0), 1.0)
        excl = jnp.where(seg < SUBLANES - 1, pltpu.roll(inc, SUBLANES - 1, 0), 1.0)
        total = jnp.broadcast_to(inc[0:1, :], (SUBLANES, nl))
        return s_rows, excl, total

    def chunk(ztc, base, cc):
        s_rows, excl, total = block_scan(ztc, base)
        off = excl * cc
        a = jnp.concatenate([s_rows[i] * off for i in range(groups)], axis=0)
        return a, cc * total

    seg_d = lax.broadcasted_iota(jnp.int32, (SUBLANES, kb_n), 0)
    lane_d = lax.broadcasted_iota(jnp.int32, (SUBLANES, kb_n), 1)
    base_d = lane_d - seg_d * groups

    def super_tile(kb0, qb, carries, diag):
        k_cat = jnp.concatenate([k0_s[kb0 + j] for j in range(U)] + [k1_s[kb0 + j] for j in range(U)], axis=0)
        zt = _dot_nt(k_cat, qb)
        vts = [vt_s[kb0 + j] for j in range(U)]
        out = []
        for h in range(2):
            carry = carries[h]
            a_blocks = [None] * U
            for j in reversed(range(U)):
                r0 = (h * U + j) * kb_n
                if not diag:
                    a, carry = chunk(zt[r0:r0 + kb_n, :], None, carry)
                else:
                    l0 = j * kb_n
                    parts, cparts = [], []
                    if l0 > 0:
                        parts.append(jnp.zeros((kb_n, l0), F32))
                        cparts.append(carry[:, 0:l0])
                    a_m, c_m = chunk(zt[r0:r0 + kb_n, l0:l0 + kb_n], base_d, carry[:, l0:l0 + kb_n])
                    parts.append(a_m)
                    cparts.append(c_m)
                    if l0 + kb_n < tq:
                        a_u, c_u = chunk(zt[r0:r0 + kb_n, l0 + kb_n:tq], None, carry[:, l0 + kb_n:tq])
                        parts.append(a_u)
                        cparts.append(c_u)
                    a = jnp.concatenate(parts, axis=1)
                    carry = jnp.concatenate(cparts, axis=1)
                a_blocks[j] = a.astype(BF16)
            a_cat = jnp.concatenate(a_blocks, axis=0)
            v_cat = jnp.concatenate([vts[j][h * hd:(h + 1) * hd, :] for j in range(U)], axis=1)
            acc_s[h] += _dot(v_cat, a_cat)
            out.append(carry)
        return tuple(out)

    def qblock(qi, _):
        q0 = pl.multiple_of(qi * tq, tq)
        qb = (q_ref[pl.ds(q0, tq), :].astype(F32) * (-(hd ** -0.5) * LOG2E)).astype(BF16)
        acc_s[...] = jnp.zeros_like(acc_s)
        one = jnp.ones((SUBLANES, tq), F32)
        carries = super_tile(qi * U, qb, (one, one), True)
        lax.fori_loop(0, qi, lambda n, c: super_tile((qi - 1 - n) * U, qb, c, False), carries)
        o = jnp.concatenate([acc_s[0], acc_s[1]], axis=0).T
        zg = z_ref[pl.ds(q0, tq), :].astype(F32)
        o_ref[pl.ds(q0, tq), :] = (o * (zg * _sigmoid(zg))).astype(BF16)
        return 0

    lax.fori_loop(0, seq // tq, qblock, 0)


def _sb_attention(proj, batch, seq):
    pairs = SB_WIDTH // LANES
    n_kb = seq // LANES
    blk = lambda off: pl.BlockSpec((seq, LANES), lambda b, p, off=off: (b, off + p))
    return pl.pallas_call(
        functools.partial(_sb_kernel, seq=seq),
        grid=(batch, pairs),
        in_specs=[blk(0), blk(pairs), blk(2 * pairs), blk(3 * pairs)],
        out_specs=pl.BlockSpec((seq, LANES), lambda b, p: (b, p)),
        out_shape=jax.ShapeDtypeStruct((batch * seq, SB_WIDTH), BF16),
        scratch_shapes=[
            pltpu.VMEM((n_kb, LANES, LANES), BF16),
            pltpu.VMEM((n_kb, LANES, LANES), BF16),
            pltpu.VMEM((n_kb, LANES, LANES), BF16),
            pltpu.VMEM((2, SB_HEAD_DIM, SB_TQ), F32),
        ],
        compiler_params=pltpu.CompilerParams(
            dimension_semantics=("arbitrary", "arbitrary"), vmem_limit_bytes=VMEM_LIMIT),
        name="sb_attention",
    )(proj, proj, proj, proj)


def _ssd_kernel(z_ref, xbc_ref, xprev_ref, sm_ref, cw_ref, cb_ref, dtb_ref, alog_ref, dexp_ref, nw_ref,
                e16_ref, tri_ref, shift_ref, o_ref, h_s):
    L = CHUNK
    gw = SSD_WIDTH // SSD_GROUPS
    hpg = SSD_HEADS // SSD_GROUPS
    n = SSD_STATE

    first = pl.program_id(1) == 0

    @pl.when(first)
    def _():
        h_s[...] = jnp.zeros_like(h_s)

    cur = xbc_ref[...]
    prev = xprev_ref[...]
    prev = jnp.where(first, jnp.zeros_like(prev), prev)
    both = jnp.concatenate([prev, cur], axis=0)
    shifted = _dot(shift_ref[...], both)
    acc = cb_ref[...] + cw_ref[SSD_CONV - 1:SSD_CONV, :] * cur.astype(F32)
    for j in range(SSD_CONV - 1):
        acc = acc + cw_ref[j:j + 1, :] * shifted[j * L:(j + 1) * L, :]
    xbc = acc * _sigmoid(acc)
    xs = xbc[:, 0:SSD_WIDTH]

    lane = lax.broadcasted_iota(jnp.int32, (L, LANES), 1)
    head_lanes = lane < SSD_HEADS
    dt = jnp.where(head_lanes, _softplus(sm_ref[...] + dtb_ref[...]), 0.0)
    a = dt * (-jnp.exp(alog_ref[...]))
    tri = tri_ref[...]
    a1, a2, a3 = _split3(a)
    cs = _dot(tri, a1) + _dot(tri, a2) + _dot(tri, a3)
    cs_t = cs.T
    ecs = jnp.exp(cs)
    dend = jnp.exp(cs[L - 1:L, :] - cs)
    stack = jnp.concatenate([dt, dt * dend, ecs], axis=0)
    s_hi, s_lo = _split2(stack)
    e16 = e16_ref[...]
    ex = _dot(s_hi, e16) + _dot(s_lo, e16)
    x_dt = (xs * ex[0:L, :]).astype(BF16)
    x_st = (xs * ex[L:2 * L, :]).astype(BF16)
    ecs_x = ex[2 * L:3 * L, :]

    rowi = lax.broadcasted_iota(jnp.int32, (L, L), 0)
    coli = lax.broadcasted_iota(jnp.int32, (L, L), 1)
    causal = rowi >= coli
    lo_lanes = coli < SSD_HEAD_DIM

    y_parts = []
    for g in range(SSD_GROUPS):
        bm = xbc[:, SSD_WIDTH + g * n:SSD_WIDTH + (g + 1) * n].astype(BF16)
        cm = xbc[:, SSD_WIDTH + SSD_GROUPS * n + g * n:SSD_WIDTH + SSD_GROUPS * n + (g + 1) * n].astype(BF16)
        scores = _dot_nt(cm, bm)
        for pr in range(hpg // 2):
            col = g * gw + pr * LANES
            xp = x_dt[:, col:col + LANES]
            yd = None
            for half in range(2):
                h = g * hpg + 2 * pr + half
                sg = cs[:, h:h + 1] - cs_t[h:h + 1, :]
                dec = jnp.where(causal, jnp.exp(jnp.where(causal, sg, 0.0)), 0.0)
                w = (scores * dec).astype(BF16)
                xm = jnp.where(lo_lanes, xp, jnp.zeros_like(xp)) if half == 0 else jnp.where(
                    lo_lanes, jnp.zeros_like(xp), xp)
                t = _dot(w, xm)
                yd = t if yd is None else yd + t
            y_parts.append(yd)
        h_prev = h_s[g]
        y_off = _dot(cm, h_prev.astype(BF16)) * ecs_x[:, g * gw:(g + 1) * gw]
        y_parts.append(y_off)
        st_new = _dot_tn(bm, x_st[:, g * gw:(g + 1) * gw])
        h_s[g] = h_prev * ecs_x[L - 1:L, g * gw:(g + 1) * gw] + st_new

    per_g = 1 + hpg // 2
    outs = []
    zg = z_ref[...].astype(F32)
    for g in range(SSD_GROUPS):
        yd = jnp.concatenate(y_parts[g * per_g:g * per_g + hpg // 2], axis=1)
        y = yd + y_parts[g * per_g + hpg // 2] + dexp_ref[:, g * gw:(g + 1) * gw] * xs[:, g * gw:(g + 1) * gw]
        zz = zg[:, g * gw:(g + 1) * gw]
        y = y * (zz * _sigmoid(zz))
        ms = jnp.mean(y * y, axis=-1, keepdims=True)
        outs.append(y * lax.rsqrt(ms + NORM_EPS) * nw_ref[:, g * gw:(g + 1) * gw])
    o_ref[...] = jnp.concatenate(outs, axis=1).astype(BF16)


def _ssd(proj, small, cw, cb, dtb, alog, dexp, nw, e16, tri, shift, batch, seq):
    nc = seq // CHUNK
    row = lambda b, c: b * nc + c
    const = lambda shape: pl.BlockSpec(shape, lambda b, c: (0,) * len(shape))
    return pl.pallas_call(
        _ssd_kernel,
        grid=(batch, nc),
        in_specs=[
            pl.BlockSpec((CHUNK, SSD_WIDTH), lambda b, c: (row(b, c), OFF_SSD_Z // SSD_WIDTH)),
            pl.BlockSpec((CHUNK, SSD_CONV_DIM), lambda b, c: (row(b, c), OFF_SSD_XBC // SSD_CONV_DIM)),
            pl.BlockSpec((CHUNK, SSD_CONV_DIM),
                         lambda b, c: (row(b, jnp.maximum(c - 1, 0)), OFF_SSD_XBC // SSD_CONV_DIM)),
            pl.BlockSpec((CHUNK, LANES), lambda b, c: (row(b, c), 0)),
            const((SSD_CONV, SSD_CONV_DIM)), const((1, SSD_CONV_DIM)), const((1, LANES)), const((1, LANES)),
            const((1, SSD_WIDTH)), const((1, SSD_WIDTH)), const((LANES, SSD_WIDTH)), const((CHUNK, CHUNK)),
            const(((SSD_CONV - 1) * CHUNK, 2 * CHUNK)),
        ],
        out_specs=pl.BlockSpec((CHUNK, SSD_WIDTH), lambda b, c: (row(b, c), 0)),
        out_shape=jax.ShapeDtypeStruct((batch * seq, SSD_WIDTH), BF16),
        scratch_shapes=[
            pltpu.VMEM((SSD_GROUPS, SSD_STATE, SSD_WIDTH // SSD_GROUPS), F32),
        ],
        compiler_params=pltpu.CompilerParams(
            dimension_semantics=("arbitrary", "arbitrary"), vmem_limit_bytes=VMEM_LIMIT),
        name="ssd",
    )(proj, proj, proj, small, cw, cb, dtb, alog, dexp, nw, e16, tri, shift)


def _gla_kernel(q_ref, k_ref, v_ref, z_ref, sm_ref, gw_ref, gb_ref, nw_ref, tri_ref, sel_ref,
                msk_ref, o_ref, st_s, att_s):
    L = CHUNK
    kk = GLA_HEAD_K
    vv = GLA_HEAD_V

    @pl.when(pl.program_id(1) == 0)
    def _():
        st_s[...] = jnp.zeros_like(st_s)

    s_hi, s_lo = _split2(sm_ref[...])
    g_hi, g_lo = _split2(gw_ref[...])
    u = _dot(s_hi, g_hi) + _dot(s_lo, g_hi) + _dot(s_hi, g_lo) + gb_ref[...]
    lg = _log_sigmoid(u) * (1.0 / GLA_GATE_TAU)
    tri = tri_ref[...]
    l1, l2, l3 = _split3(lg)
    gc = _dot(tri, l1) + _dot(tri, l2) + _dot(tri, l3)

    q = q_ref[...].astype(F32) * (kk ** -0.5)
    k = k_ref[...].astype(F32)
    lane = lax.broadcasted_iota(jnp.int32, (L, GLA_KEY), 1)
    rowi = lax.broadcasted_iota(jnp.int32, (L, GLA_KEY), 0)
    head_of_lane = lane // kk

    def head_mask(xb, h):
        return jnp.where(head_of_lane == h, xb, jnp.zeros_like(xb))

    g_last = gc[L - 1:L, :]
    q_in = (q * jnp.exp(gc)).astype(BF16)
    k_dec = (k * jnp.exp(g_last - gc)).astype(BF16)

    k_growth = jnp.max(jnp.abs(k), axis=0, keepdims=True) * jnp.exp(-g_last)
    direct = jnp.max(k_growth) < GLA_DIRECT_MAX

    @pl.when(direct)
    def _():
        k_up = (k * jnp.exp(-gc)).astype(BF16)
        m = msk_ref[GLA_LEVELS + 1] > 0.0
        for h in range(GLA_HEADS):
            att_s[h] = jnp.where(m, _dot_nt(head_mask(q_in, h), k_up), 0.0)

    @pl.when(jnp.logical_not(direct))
    def _():
        c1, c2, c3 = _split3(gc)
        sel = sel_ref[...]
        ref_g = _dot(sel, c1) + _dot(sel, c2) + _dot(sel, c3)
        att = [None] * GLA_HEADS
        for lv in range(GLA_LEVELS):
            expo = -jnp.abs(gc - ref_g[lv * L:(lv + 1) * L, :])
            upper = ((rowi >> lv) & 1) == 1
            xb = (jnp.where(upper, q, k) * jnp.exp(expo)).astype(BF16)
            m = msk_ref[lv]
            for h in range(GLA_HEADS):
                p = _dot_nt(head_mask(xb, h), xb) * m
                att[h] = p if att[h] is None else att[h] + p
        qb = q.astype(BF16)
        kb = k.astype(BF16)
        m = msk_ref[GLA_LEVELS]
        for h in range(GLA_HEADS):
            att_s[h] = att[h] + _dot_nt(head_mask(qb, h), kb) * m

    st = st_s[...]
    st_b = st.astype(BF16)
    st_new = st * jnp.exp(g_last)
    lane_s = lax.broadcasted_iota(jnp.int32, (vv, GLA_KEY), 1) // kk
    outs = []
    for h in range(GLA_HEADS):
        vh = v_ref[:, h * vv:(h + 1) * vv]
        o = _dot_nt(head_mask(q_in, h), st_b) + _dot(att_s[h].astype(BF16), vh)
        upd = _dot_tn(vh, k_dec)
        st_new = st_new + jnp.where(lane_s == h, upd, 0.0)
        ms = jnp.mean(o * o, axis=-1, keepdims=True)
        o = o * lax.rsqrt(ms + NORM_EPS) * nw_ref[...]
        zz = z_ref[:, h * vv:(h + 1) * vv].astype(F32)
        outs.append(o * (zz * _sigmoid(zz)))
    st_s[...] = st_new
    o_ref[...] = jnp.concatenate(outs, axis=1).astype(BF16)


def _gla(proj, small, gw, gb, nw, tri, sel, msk, batch, seq):
    nc = seq // CHUNK
    row = lambda b, c: b * nc + c
    const = lambda shape: pl.BlockSpec(shape, lambda b, c: (0,) * len(shape))
    return pl.pallas_call(
        _gla_kernel,
        grid=(batch, nc),
        in_specs=[
            pl.BlockSpec((CHUNK, GLA_KEY), lambda b, c: (row(b, c), OFF_GLA_Q // GLA_KEY)),
            pl.BlockSpec((CHUNK, GLA_KEY), lambda b, c: (row(b, c), OFF_GLA_K // GLA_KEY)),
            pl.BlockSpec((CHUNK, GLA_WIDTH), lambda b, c: (row(b, c), OFF_GLA_V // GLA_WIDTH)),
            pl.BlockSpec((CHUNK, GLA_WIDTH), lambda b, c: (row(b, c), OFF_GLA_Z // GLA_WIDTH)),
            pl.BlockSpec((CHUNK, LANES), lambda b, c: (row(b, c), 0)),
            const((LANES, GLA_KEY)), const((1, GLA_KEY)), const((1, GLA_HEAD_V)),
            const((CHUNK, CHUNK)), const((GLA_LEVELS * CHUNK, CHUNK)), const((GLA_LEVELS + 2, CHUNK, CHUNK)),
        ],
        out_specs=pl.BlockSpec((CHUNK, GLA_WIDTH), lambda b, c: (row(b, c), 0)),
        out_shape=jax.ShapeDtypeStruct((batch * seq, GLA_WIDTH), BF16),
        scratch_shapes=[pltpu.VMEM((GLA_HEAD_V, GLA_KEY), F32),
                        pltpu.VMEM((GLA_HEADS, CHUNK, CHUNK), F32)],
        compiler_params=pltpu.CompilerParams(
            dimension_semantics=("arbitrary", "arbitrary"), vmem_limit_bytes=VMEM_LIMIT),
        name="gla",
    )(proj, proj, proj, proj, small, gw, gb, nw, tri, sel, msk)


def _outproj_kernel(x_ref, ya_ref, yb_ref, yc_ref, w_ref, fw_ref, o_ref, *, final):
    y = _dot(ya_ref[...], w_ref[0:SB_WIDTH, :])
    y = y + _dot(yb_ref[...], w_ref[SB_WIDTH:SB_WIDTH + SSD_WIDTH, :])
    y = y + _dot(yc_ref[...], w_ref[SB_WIDTH + SSD_WIDTH:D_INNER, :])
    xn = x_ref[...] + y
    if final:
        ms = jnp.mean(xn * xn, axis=-1, keepdims=True)
        xn = xn * lax.rsqrt(ms + NORM_EPS) * fw_ref[...]
    o_ref[...] = xn


def _outproj(x2, ya, yb, yc, w, fw, *, final, tm=512):
    t = x2.shape[0]
    rows = lambda width: pl.BlockSpec((tm, width), lambda i: (i, 0))
    return pl.pallas_call(
        functools.partial(_outproj_kernel, final=final),
        grid=(t // tm,),
        in_specs=[rows(D_MODEL), rows(SB_WIDTH), rows(SSD_WIDTH), rows(GLA_WIDTH),
                  pl.BlockSpec((D_INNER, D_MODEL), lambda i: (0, 0)),
                  pl.BlockSpec((1, D_MODEL), lambda i: (0, 0))],
        out_specs=rows(D_MODEL),
        out_shape=jax.ShapeDtypeStruct((t, D_MODEL), F32),
        compiler_params=pltpu.CompilerParams(
            dimension_semantics=("arbitrary",), vmem_limit_bytes=VMEM_LIMIT),
        name="outproj_final" if final else "outproj",
    )(x2, ya, yb, yc, w, fw)


def _constants():
    L = CHUNK
    t = np.arange(L)
    tri = (t[:, None] >= t[None, :]).astype(np.float32)
    sel = np.zeros((GLA_LEVELS * L, L), np.float32)
    msk = np.zeros((GLA_LEVELS + 2, L, L), np.float32)
    msk[GLA_LEVELS + 1] = tri
    shift = np.zeros(((SSD_CONV - 1) * L, 2 * L), np.float32)
    for j in range(SSD_CONV - 1):
        shift[j * L + t, L + t - (SSD_CONV - 1 - j)] = 1.0
    for lv in range(GLA_LEVELS):
        mid = ((t >> (lv + 1)) << (lv + 1)) + (1 << lv)
        sel[lv * L + t, mid - 1] = 1.0
        same = (t[:, None] >> (lv + 1)) == (t[None, :] >> (lv + 1))
        up = ((t[:, None] >> lv) & 1) == 1
        low = ((t[None, :] >> lv) & 1) == 0
        msk[lv] = (same & up & low).astype(np.float32)
    msk[GLA_LEVELS] = np.eye(L, dtype=np.float32)
    e16 = np.zeros((LANES, SSD_WIDTH), np.float32)
    for h in range(SSD_HEADS):
        e16[h, h * SSD_HEAD_DIM:(h + 1) * SSD_HEAD_DIM] = 1.0
    return (jnp.asarray(tri, BF16), jnp.asarray(sel, BF16), jnp.asarray(msk, F32), jnp.asarray(e16, BF16),
            jnp.asarray(shift, BF16))


def _pad_lanes(v, offset=0):
    out = jnp.zeros((1, LANES), F32)
    return out.at[0, offset:offset + v.shape[0]].set(v.astype(F32))


def kernel(x, norm_w, w_in, ssd_conv_w, ssd_conv_b, ssd_dt_bias, ssd_a_log, ssd_d, ssd_norm_w,
           gla_gate_w, gla_gate_b, gla_norm_w, w_out, final_norm_w):
    batch, seq, _ = x.shape
    assert seq % 256 == 0 and (batch * seq) % 512 == 0
    tri, sel, msk, e16, shift = _constants()
    x2 = x.reshape(batch * seq, D_MODEL).astype(F32)
    dt_col = OFF_GLA_Q
    glr_col = dt_col + SSD_HEADS + 2 * GLA_KEY + 2 * GLA_WIDTH
    for layer in range(DEPTH):
        w = w_in[layer]
        w_main = jnp.concatenate([w[:, :dt_col], w[:, dt_col + SSD_HEADS:glr_col]], axis=1).astype(BF16)
        w_small = jnp.zeros((D_MODEL, LANES), F32)
        w_small = w_small.at[:, 0:SSD_HEADS].set(w[:, dt_col:dt_col + SSD_HEADS])
        w_small = w_small.at[:, SSD_HEADS:SSD_HEADS + GLA_GATE_RANK].set(w[:, glr_col:glr_col + GLA_GATE_RANK])
        proj, small = _inproj(x2, norm_w[layer].reshape(1, D_MODEL), w_main, w_small.astype(BF16))

        ya = _sb_attention(proj, batch, seq)
        yb = _ssd(proj, small, ssd_conv_w[layer], ssd_conv_b[layer].reshape(1, SSD_CONV_DIM),
                  _pad_lanes(ssd_dt_bias[layer]), _pad_lanes(ssd_a_log[layer]),
                  jnp.repeat(ssd_d[layer], SSD_HEAD_DIM).reshape(1, SSD_WIDTH),
                  ssd_norm_w[layer].reshape(1, SSD_WIDTH), e16, tri, shift, batch, seq)
        gw = jnp.zeros((LANES, GLA_KEY), F32).at[SSD_HEADS:SSD_HEADS + GLA_GATE_RANK, :].set(gla_gate_w[layer])
        yc = _gla(proj, small, gw, gla_gate_b[layer].reshape(1, GLA_KEY),
                  gla_norm_w[layer].reshape(1, GLA_HEAD_V), tri, sel, msk, batch, seq)
        x2 = _outproj(x2, ya, yb, yc, w_out[layer].astype(BF16), final_norm_w.reshape(1, D_MODEL),
                      final=(layer == DEPTH - 1))
    return x2.reshape(batch, seq, D_MODEL).astype(x.dtype)
```

```python
import functools

import jax
import jax.numpy as jnp
import numpy as np
from jax import lax
from jax.experimental import pallas as pl
from jax.experimental.pallas import tpu as pltpu

F32 = jnp.float32
BF16 = jnp.bfloat16

D_MODEL = 1024
DEPTH = 2
D_INNER = 2 * D_MODEL
SB_WIDTH = D_INNER // 4
SB_HEAD_DIM = 64
SSD_WIDTH = D_INNER // 2
SSD_HEAD_DIM = 64
SSD_HEADS = SSD_WIDTH // SSD_HEAD_DIM
SSD_GROUPS = 2
SSD_STATE = 128
SSD_CONV = 4
SSD_CONV_DIM = SSD_WIDTH + 2 * SSD_GROUPS * SSD_STATE
GLA_WIDTH = D_INNER // 4
GLA_HEADS = 4
GLA_KEY = GLA_WIDTH // 2
GLA_HEAD_K = GLA_KEY // GLA_HEADS
GLA_HEAD_V = GLA_WIDTH // GLA_HEADS
GLA_GATE_RANK = 16
GLA_GATE_TAU = 16.0
NORM_EPS = 1e-6

LANES = 128
SUBLANES = 8
CHUNK = 128
SSD_SUBCHUNKS = 2
GLA_SUBCHUNKS = 2
N_MAIN = 4 * SB_WIDTH + SSD_WIDTH + SSD_CONV_DIM + 2 * GLA_KEY + 2 * GLA_WIDTH
OFF_SSD_Z = 4 * SB_WIDTH
OFF_SSD_XBC = OFF_SSD_Z + SSD_WIDTH
OFF_GLA_Q = OFF_SSD_XBC + SSD_CONV_DIM
OFF_GLA_K = OFF_GLA_Q + GLA_KEY
OFF_GLA_V = OFF_GLA_K + GLA_KEY
OFF_GLA_Z = OFF_GLA_V + GLA_WIDTH
GLA_LEVELS = 7
GLA_DIRECT_MAX = 1e30
SB_TQ = 512
LOG2E = 1.4426950408889634
VMEM_LIMIT = 56 * 1024 * 1024


def _dot(a, b):
    return jnp.dot(a, b, preferred_element_type=F32)


def _dot_nt(a, b):
    return lax.dot_general(a, b, (((1,), (1,)), ((), ())), preferred_element_type=F32)


def _dot_tn(a, b):
    return lax.dot_general(a, b, (((0,), (0,)), ((), ())), preferred_element_type=F32)


def _split2(x):
    hi = x.astype(BF16)
    lo = (x - hi.astype(F32)).astype(BF16)
    return hi, lo


def _split3(x):
    hi = x.astype(BF16)
    r = x - hi.astype(F32)
    mid = r.astype(BF16)
    lo = (r - mid.astype(F32)).astype(BF16)
    return hi, mid, lo


def _sigmoid(x):
    return 1.0 / (1.0 + jnp.exp2(x * (-LOG2E)))


def _log_sigmoid(x):
    return jnp.minimum(x, 0.0) - jnp.log(1.0 + jnp.exp(-jnp.abs(x)))


def _softplus(x):
    return jnp.maximum(x, 0.0) + jnp.log(1.0 + jnp.exp(-jnp.abs(x)))


def _inproj_kernel(x_ref, nw_ref, w_ref, ws_ref, o_ref, os_ref, *, n_chunk):
    x = x_ref[...]
    ms = jnp.mean(x * x, axis=-1, keepdims=True)
    h = (x * lax.rsqrt(ms + NORM_EPS) * nw_ref[...]).astype(BF16)
    for c0 in range(0, N_MAIN, n_chunk):
        o_ref[:, c0:c0 + n_chunk] = _dot(h, w_ref[:, c0:c0 + n_chunk]).astype(BF16)
    os_ref[...] = _dot(h, ws_ref[...])


def _inproj(x2, nw, w_main, w_small, *, tm=512, n_chunk=1536):
    t = x2.shape[0]
    return pl.pallas_call(
        functools.partial(_inproj_kernel, n_chunk=n_chunk),
        grid=(t // tm,),
        in_specs=[
            pl.BlockSpec((tm, D_MODEL), lambda i: (i, 0)),
            pl.BlockSpec((1, D_MODEL), lambda i: (0, 0)),
            pl.BlockSpec((D_MODEL, N_MAIN), lambda i: (0, 0)),
            pl.BlockSpec((D_MODEL, LANES), lambda i: (0, 0)),
        ],
        out_specs=[
            pl.BlockSpec((tm, N_MAIN), lambda i: (i, 0)),
            pl.BlockSpec((tm, LANES), lambda i: (i, 0)),
        ],
        out_shape=[
            jax.ShapeDtypeStruct((t, N_MAIN), BF16),
            jax.ShapeDtypeStruct((t, LANES), F32),
        ],
        compiler_params=pltpu.CompilerParams(
            dimension_semantics=("arbitrary",), vmem_limit_bytes=VMEM_LIMIT),
        name="inproj",
    )(x2, nw, w_main, w_small)


def _sb_kernel(q_ref, k_ref, v_ref, z_ref, o_ref, k0_s, k1_s, vt_s, acc_s, zt_s, a_s, qs_s, *, seq):
    kb_n = LANES
    n_kb = seq // kb_n
    groups = kb_n // SUBLANES
    hd = SB_HEAD_DIM
    tq = SB_TQ
    U = tq // kb_n

    lane = lax.broadcasted_iota(jnp.int32, (kb_n, kb_n), 1)
    row = lax.broadcasted_iota(jnp.int32, (kb_n, kb_n), 0)
    src = (row % SUBLANES) * groups + row // SUBLANES
    perm = jnp.where(lane == src, 1.0, 0.0).astype(BF16)
    src_t = (lane % SUBLANES) * groups + lane // SUBLANES
    perm_t = jnp.where(row == src_t, 1.0, 0.0).astype(BF16)
    lo_lanes = lane < hd

    def prep(n, carry):
        for j in range(U):
            kb = n * U + j
            r0 = pl.multiple_of(kb * kb_n, kb_n)
            kp = _dot(perm, k_ref[pl.ds(r0, kb_n), :]).astype(BF16)
            k0_s[kb] = jnp.where(lo_lanes, kp, jnp.zeros_like(kp))
            k1_s[kb] = jnp.where(lo_lanes, jnp.zeros_like(kp), kp)
            vt = v_ref[pl.ds(r0, kb_n), :].astype(F32).T.astype(BF16)
            vt_s[kb] = _dot(vt, perm_t).astype(BF16)
            qs_s[pl.ds(r0, kb_n), :] = (
                q_ref[pl.ds(r0, kb_n), :].astype(F32) * (-(hd ** -0.5) * LOG2E)).astype(BF16)
        return carry

    lax.fori_loop(0, n_kb // U, prep, 0)

    def block_scan(zt, base):
        nl = zt.shape[1]
        seg = lax.broadcasted_iota(jnp.int32, (SUBLANES, nl), 0)
        run = jnp.ones((SUBLANES, nl), F32)
        s_rows = [None] * groups
        for i in reversed(range(groups)):
            zn = zt[i * SUBLANES:(i + 1) * SUBLANES, :]
            beta = 1.0 / (1.0 + jnp.exp2(zn))
            if base is not None:
                beta = jnp.where(i < base, beta, 0.0)
            s_rows[i] = beta * run
            run = run - s_rows[i]
        inc = run
        for sh in (1, 2, 4):
            inc = inc * jnp.where(seg < SUBLANES - sh, pltpu.roll(inc, SUBLANES - sh, 0), 1.0)
        excl = jnp.where(seg < SUBLANES - 1, pltpu.roll(inc, SUBLANES - 1, 0), 1.0)
        total = jnp.broadcast_to(inc[0:1, :], (SUBLANES, nl))
        return s_rows, excl, total

    def chunk(ztc, base, cc):
        s_rows, excl, total = block_scan(ztc, base)
        off = excl * cc
        a = jnp.concatenate([s_rows[i] * off for i in range(groups)], axis=0)
        return a, cc * total

    seg_d = lax.broadcasted_iota(jnp.int32, (SUBLANES, kb_n), 0)
    lane_d = lax.broadcasted_iota(jnp.int32, (SUBLANES, kb_n), 1)
    base_d = lane_d - seg_d * groups

    n_q = seq // tq

    def issue_scores(kb0, qi, slot):
        qb = qs_s[qi * tq:(qi + 1) * tq, :]
        k_cat = jnp.concatenate([k0_s[kb0 + j] for j in range(U)] + [k1_s[kb0 + j] for j in range(U)], axis=0)
        zt_s[slot] = _dot_nt(k_cat, qb)

    def add_values(kb0, slot):
        vts = [vt_s[kb0 + j] for j in range(U)]
        for h in range(2):
            v_cat = jnp.concatenate([vts[j][h * hd:(h + 1) * hd, :] for j in range(U)], axis=1)
            acc_s[h] += _dot(v_cat, a_s[slot, h])

    def finish_qblock(qi):
        q0 = qi * tq
        o = jnp.concatenate([acc_s[0], acc_s[1]], axis=0).T
        zg = z_ref[pl.ds(q0, tq), :].astype(F32)
        o_ref[pl.ds(q0, tq), :] = (o * (zg * _sigmoid(zg))).astype(BF16)
        acc_s[...] = jnp.zeros_like(acc_s)

    def weights(slot, carries, diag):
        out = []
        for h in range(2):
            carry = carries[h]
            for j in reversed(range(U)):
                r0 = (h * U + j) * kb_n
                if not diag:
                    a, carry = chunk(zt_s[slot, r0:r0 + kb_n, :], None, carry)
                else:
                    l0 = j * kb_n
                    parts, cparts = [], []
                    if l0 > 0:
                        parts.append(jnp.zeros((kb_n, l0), F32))
                        cparts.append(carry[:, 0:l0])
                    a_m, c_m = chunk(zt_s[slot, r0:r0 + kb_n, l0:l0 + kb_n], base_d, carry[:, l0:l0 + kb_n])
                    parts.append(a_m)
                    cparts.append(c_m)
                    if l0 + kb_n < tq:
                        a_u, c_u = chunk(zt_s[slot, r0:r0 + kb_n, l0 + kb_n:tq], None, carry[:, l0 + kb_n:tq])
                        parts.append(a_u)
                        cparts.append(c_u)
                    a = jnp.concatenate(parts, axis=1)
                    carry = jnp.concatenate(cparts, axis=1)
                a_s[slot, h, j * kb_n:(j + 1) * kb_n, :] = a.astype(BF16)
            out.append(carry)
        return tuple(out)

    items = []
    for qi in range(n_q):
        items.append((qi, qi * U, True))
        items.extend((qi, j * U, False) for j in reversed(range(qi)))

    acc_s[...] = jnp.zeros_like(acc_s)
    issue_scores(items[0][1], items[0][0], 0)
    carries = None
    for n, (qi, kb0, diag) in enumerate(items):
        if n + 1 < len(items):
            issue_scores(items[n + 1][1], items[n + 1][0], (n + 1) % 2)
        if n > 0:
            add_values(items[n - 1][1], (n - 1) % 2)
        if diag:
            if qi > 0:
                finish_qblock(qi - 1)
            one = jnp.ones((SUBLANES, tq), F32)
            carries = (one, one)
        carries = weights(n % 2, carries, diag)
    add_values(items[-1][1], (len(items) - 1) % 2)
    finish_qblock(n_q - 1)


def _sb_attention(proj, batch, seq):
    pairs = SB_WIDTH // LANES
    n_kb = seq // LANES
    blk = lambda off: pl.BlockSpec((seq, LANES), lambda b, p, off=off: (b, off + p))
    return pl.pallas_call(
        functools.partial(_sb_kernel, seq=seq),
        grid=(batch, pairs),
        in_specs=[blk(0), blk(pairs), blk(2 * pairs), blk(3 * pairs)],
        out_specs=pl.BlockSpec((seq, LANES), lambda b, p: (b, p)),
        out_shape=jax.ShapeDtypeStruct((batch * seq, SB_WIDTH), BF16),
        scratch_shapes=[
            pltpu.VMEM((n_kb, LANES, LANES), BF16),
            pltpu.VMEM((n_kb, LANES, LANES), BF16),
            pltpu.VMEM((n_kb, LANES, LANES), BF16),
            pltpu.VMEM((2, SB_HEAD_DIM, SB_TQ), F32),
            pltpu.VMEM((2, 2 * SB_TQ, SB_TQ), F32),
            pltpu.VMEM((2, 2, SB_TQ, SB_TQ), BF16),
            pltpu.VMEM((seq, LANES), BF16),
        ],
        compiler_params=pltpu.CompilerParams(
            dimension_semantics=("arbitrary", "arbitrary"), vmem_limit_bytes=VMEM_LIMIT),
        name="sb_attention",
    )(proj, proj, proj, proj)


def _ssd_kernel(z_ref, xbc_ref, xprev_ref, sm_ref, cw_ref, cb_ref, dtb_ref, alog_ref, dexp_ref, nw_ref,
                e16_ref, tri_ref, shift_ref, o_ref, h_s):
    first = pl.program_id(1) == 0

    @pl.when(first)
    def _():
        h_s[...] = jnp.zeros_like(h_s)

    for sub in range(SSD_SUBCHUNKS):
        _ssd_chunk(sub, first, z_ref, xbc_ref, xprev_ref, sm_ref, cw_ref, cb_ref, dtb_ref, alog_ref,
                   dexp_ref, nw_ref, e16_ref, tri_ref, shift_ref, o_ref, h_s)


def _ssd_chunk(sub, first, z_ref, xbc_ref, xprev_ref, sm_ref, cw_ref, cb_ref, dtb_ref, alog_ref,
               dexp_ref, nw_ref, e16_ref, tri_ref, shift_ref, o_ref, h_s):
    L = CHUNK
    gw = SSD_WIDTH // SSD_GROUPS
    hpg = SSD_HEADS // SSD_GROUPS
    n = SSD_STATE
    rows = slice(sub * L, (sub + 1) * L)

    cur = xbc_ref[rows, :]
    if sub == 0:
        prev = xprev_ref[...]
        prev = jnp.where(first, jnp.zeros_like(prev), prev)
    else:
        prev = xbc_ref[(sub - 1) * L:sub * L, :]
    both = jnp.concatenate([prev, cur], axis=0)
    shifted = _dot(shift_ref[...], both)
    acc = cb_ref[...] + cw_ref[SSD_CONV - 1:SSD_CONV, :] * cur.astype(F32)
    for j in range(SSD_CONV - 1):
        acc = acc + cw_ref[j:j + 1, :] * shifted[j * L:(j + 1) * L, :]
    xbc = acc * _sigmoid(acc)
    xs = xbc[:, 0:SSD_WIDTH]

    lane = lax.broadcasted_iota(jnp.int32, (L, LANES), 1)
    head_lanes = lane < SSD_HEADS
    dt = jnp.where(head_lanes, _softplus(sm_ref[rows, :] + dtb_ref[...]), 0.0)
    a = dt * (-jnp.exp(alog_ref[...]))
    tri = tri_ref[...]
    a1, a2, a3 = _split3(a)
    cs = _dot(tri, a1) + _dot(tri, a2) + _dot(tri, a3)
    cs_t = cs.T
    ecs = jnp.exp(cs)
    dend = jnp.exp(cs[L - 1:L, :] - cs)
    stack = jnp.concatenate([dt, dt * dend, ecs], axis=0)
    s_hi, s_lo = _split2(stack)
    e16 = e16_ref[...]
    ex = _dot(s_hi, e16) + _dot(s_lo, e16)
    x_dt = (xs * ex[0:L, :]).astype(BF16)
    x_st = (xs * ex[L:2 * L, :]).astype(BF16)
    ecs_x = ex[2 * L:3 * L, :]

    rowi = lax.broadcasted_iota(jnp.int32, (L, L), 0)
    coli = lax.broadcasted_iota(jnp.int32, (L, L), 1)
    causal = rowi >= coli
    lo_lanes = coli < SSD_HEAD_DIM

    y_parts = []
    for g in range(SSD_GROUPS):
        bm = xbc[:, SSD_WIDTH + g * n:SSD_WIDTH + (g + 1) * n].astype(BF16)
        cm = xbc[:, SSD_WIDTH + SSD_GROUPS * n + g * n:SSD_WIDTH + SSD_GROUPS * n + (g + 1) * n].astype(BF16)
        scores = _dot_nt(cm, bm)
        for pr in range(hpg // 2):
            col = g * gw + pr * LANES
            xp = x_dt[:, col:col + LANES]
            yd = None
            for half in range(2):
                h = g * hpg + 2 * pr + half
                sg = cs[:, h:h + 1] - cs_t[h:h + 1, :]
                dec = jnp.where(causal, jnp.exp(jnp.where(causal, sg, 0.0)), 0.0)
                w = (scores * dec).astype(BF16)
                xm = jnp.where(lo_lanes, xp, jnp.zeros_like(xp)) if half == 0 else jnp.where(
                    lo_lanes, jnp.zeros_like(xp), xp)
                t = _dot(w, xm)
                yd = t if yd is None else yd + t
            y_parts.append(yd)
        h_prev = h_s[g]
        y_off = _dot(cm, h_prev.astype(BF16)) * ecs_x[:, g * gw:(g + 1) * gw]
        y_parts.append(y_off)
        st_new = _dot_tn(bm, x_st[:, g * gw:(g + 1) * gw])
        h_s[g] = h_prev * ecs_x[L - 1:L, g * gw:(g + 1) * gw] + st_new

    per_g = 1 + hpg // 2
    outs = []
    zg = z_ref[rows, :].astype(F32)
    for g in range(SSD_GROUPS):
        yd = jnp.concatenate(y_parts[g * per_g:g * per_g + hpg // 2], axis=1)
        y = yd + y_parts[g * per_g + hpg // 2] + dexp_ref[:, g * gw:(g + 1) * gw] * xs[:, g * gw:(g + 1) * gw]
        zz = zg[:, g * gw:(g + 1) * gw]
        y = y * (zz * _sigmoid(zz))
        ms = jnp.mean(y * y, axis=-1, keepdims=True)
        outs.append(y * lax.rsqrt(ms + NORM_EPS) * nw_ref[:, g * gw:(g + 1) * gw])
    o_ref[rows, :] = jnp.concatenate(outs, axis=1).astype(BF16)


def _ssd(proj, small, cw, cb, dtb, alog, dexp, nw, e16, tri, shift, batch, seq):
    step = SSD_SUBCHUNKS * CHUNK
    nc = seq // step
    row = lambda b, c: b * nc + c
    prev_row = lambda b, c: b * (seq // CHUNK) + jnp.maximum(c * SSD_SUBCHUNKS - 1, 0)
    const = lambda shape: pl.BlockSpec(shape, lambda b, c: (0,) * len(shape))
    return pl.pallas_call(
        _ssd_kernel,
        grid=(batch, nc),
        in_specs=[
            pl.BlockSpec((step, SSD_WIDTH), lambda b, c: (row(b, c), OFF_SSD_Z // SSD_WIDTH)),
            pl.BlockSpec((step, SSD_CONV_DIM), lambda b, c: (row(b, c), OFF_SSD_XBC // SSD_CONV_DIM)),
            pl.BlockSpec((CHUNK, SSD_CONV_DIM), lambda b, c: (prev_row(b, c), OFF_SSD_XBC // SSD_CONV_DIM)),
            pl.BlockSpec((step, LANES), lambda b, c: (row(b, c), 0)),
            const((SSD_CONV, SSD_CONV_DIM)), const((1, SSD_CONV_DIM)), const((1, LANES)), const((1, LANES)),
            const((1, SSD_WIDTH)), const((1, SSD_WIDTH)), const((LANES, SSD_WIDTH)), const((CHUNK, CHUNK)),
            const(((SSD_CONV - 1) * CHUNK, 2 * CHUNK)),
        ],
        out_specs=pl.BlockSpec((step, SSD_WIDTH), lambda b, c: (row(b, c), 0)),
        out_shape=jax.ShapeDtypeStruct((batch * seq, SSD_WIDTH), BF16),
        scratch_shapes=[
            pltpu.VMEM((SSD_GROUPS, SSD_STATE, SSD_WIDTH // SSD_GROUPS), F32),
        ],
        compiler_params=pltpu.CompilerParams(
            dimension_semantics=("arbitrary", "arbitrary"), vmem_limit_bytes=VMEM_LIMIT),
        name="ssd",
    )(proj, proj, proj, small, cw, cb, dtb, alog, dexp, nw, e16, tri, shift)


def _gla_kernel(q_ref, k_ref, v_ref, z_ref, sm_ref, gw_ref, gb_ref, nw_ref, tri_ref, sel_ref,
                msk_ref, o_ref, st_s, att_s):
    L = CHUNK
    kk = GLA_HEAD_K
    vv = GLA_HEAD_V

    @pl.when(pl.program_id(1) == 0)
    def _():
        st_s[...] = jnp.zeros_like(st_s)

    lane = lax.broadcasted_iota(jnp.int32, (L, GLA_KEY), 1)
    rowi = lax.broadcasted_iota(jnp.int32, (L, GLA_KEY), 0)
    head_of_lane = lane // kk

    def head_mask(xb, h):
        return jnp.where(head_of_lane == h, xb, jnp.zeros_like(xb))

    g_hi, g_lo = _split2(gw_ref[...])
    tri = tri_ref[...]
    subs = []
    for sub in range(GLA_SUBCHUNKS):
        rows = slice(sub * L, (sub + 1) * L)
        s_hi, s_lo = _split2(sm_ref[rows, :])
        u = _dot(s_hi, g_hi) + _dot(s_lo, g_hi) + _dot(s_hi, g_lo) + gb_ref[...]
        lg = _log_sigmoid(u) * (1.0 / GLA_GATE_TAU)
        l1, l2, l3 = _split3(lg)
        gc = _dot(tri, l1) + _dot(tri, l2) + _dot(tri, l3)
        q = q_ref[rows, :].astype(F32) * (kk ** -0.5)
        k = k_ref[rows, :].astype(F32)
        g_last = gc[L - 1:L, :]
        q_in = (q * jnp.exp(gc)).astype(BF16)
        k_dec = (k * jnp.exp(g_last - gc)).astype(BF16)
        k_growth = jnp.max(jnp.abs(k), axis=0, keepdims=True) * jnp.exp(-g_last)
        subs.append(dict(rows=rows, gc=gc, q=q, k=k, g_last=g_last, q_in=q_in, k_dec=k_dec, growth=k_growth))

    growth = subs[0]["growth"]
    for s in subs[1:]:
        growth = jnp.maximum(growth, s["growth"])
    direct = jnp.max(growth) < GLA_DIRECT_MAX

    @pl.when(direct)
    def _():
        m = msk_ref[GLA_LEVELS + 1] > 0.0
        for i, s in enumerate(subs):
            k_up = (s["k"] * jnp.exp(-s["gc"])).astype(BF16)
            for h in range(GLA_HEADS):
                att_s[i, h] = jnp.where(m, _dot_nt(head_mask(s["q_in"], h), k_up), 0.0)

    @pl.when(jnp.logical_not(direct))
    def _():
        sel = sel_ref[...]
        for i, s in enumerate(subs):
            gc, q, k = s["gc"], s["q"], s["k"]
            c1, c2, c3 = _split3(gc)
            ref_g = _dot(sel, c1) + _dot(sel, c2) + _dot(sel, c3)
            att = [None] * GLA_HEADS
            for lv in range(GLA_LEVELS):
                expo = -jnp.abs(gc - ref_g[lv * L:(lv + 1) * L, :])
                upper = ((rowi >> lv) & 1) == 1
                xb = (jnp.where(upper, q, k) * jnp.exp(expo)).astype(BF16)
                m = msk_ref[lv]
                for h in range(GLA_HEADS):
                    p = _dot_nt(head_mask(xb, h), xb) * m
                    att[h] = p if att[h] is None else att[h] + p
            qb = q.astype(BF16)
            kb = k.astype(BF16)
            m = msk_ref[GLA_LEVELS]
            for h in range(GLA_HEADS):
                att_s[i, h] = att[h] + _dot_nt(head_mask(qb, h), kb) * m

    lane_s = lax.broadcasted_iota(jnp.int32, (vv, GLA_KEY), 1) // kk
    st = st_s[...]
    for i, s in enumerate(subs):
        rows = s["rows"]
        st_b = st.astype(BF16)
        st_new = st * jnp.exp(s["g_last"])
        outs = []
        for h in range(GLA_HEADS):
            vh = v_ref[rows, h * vv:(h + 1) * vv]
            o = _dot_nt(head_mask(s["q_in"], h), st_b) + _dot(att_s[i, h].astype(BF16), vh)
            upd = _dot_tn(vh, s["k_dec"])
            st_new = st_new + jnp.where(lane_s == h, upd, 0.0)
            ms = jnp.mean(o * o, axis=-1, keepdims=True)
            o = o * lax.rsqrt(ms + NORM_EPS) * nw_ref[...]
            zz = z_ref[rows, h * vv:(h + 1) * vv].astype(F32)
            outs.append(o * (zz * _sigmoid(zz)))
        o_ref[rows, :] = jnp.concatenate(outs, axis=1).astype(BF16)
        st = st_new
    st_s[...] = st


def _gla(proj, small, gw, gb, nw, tri, sel, msk, batch, seq):
    step = GLA_SUBCHUNKS * CHUNK
    nc = seq // step
    row = lambda b, c: b * nc + c
    const = lambda shape: pl.BlockSpec(shape, lambda b, c: (0,) * len(shape))
    return pl.pallas_call(
        _gla_kernel,
        grid=(batch, nc),
        in_specs=[
            pl.BlockSpec((step, GLA_KEY), lambda b, c: (row(b, c), OFF_GLA_Q // GLA_KEY)),
            pl.BlockSpec((step, GLA_KEY), lambda b, c: (row(b, c), OFF_GLA_K // GLA_KEY)),
            pl.BlockSpec((step, GLA_WIDTH), lambda b, c: (row(b, c), OFF_GLA_V // GLA_WIDTH)),
            pl.BlockSpec((step, GLA_WIDTH), lambda b, c: (row(b, c), OFF_GLA_Z // GLA_WIDTH)),
            pl.BlockSpec((step, LANES), lambda b, c: (row(b, c), 0)),
            const((LANES, GLA_KEY)), const((1, GLA_KEY)), const((1, GLA_HEAD_V)),
            const((CHUNK, CHUNK)), const((GLA_LEVELS * CHUNK, CHUNK)), const((GLA_LEVELS + 2, CHUNK, CHUNK)),
        ],
        out_specs=pl.BlockSpec((step, GLA_WIDTH), lambda b, c: (row(b, c), 0)),
        out_shape=jax.ShapeDtypeStruct((batch * seq, GLA_WIDTH), BF16),
        scratch_shapes=[pltpu.VMEM((GLA_HEAD_V, GLA_KEY), F32),
                        pltpu.VMEM((GLA_SUBCHUNKS, GLA_HEADS, CHUNK, CHUNK), F32)],
        compiler_params=pltpu.CompilerParams(
            dimension_semantics=("arbitrary", "arbitrary"), vmem_limit_bytes=VMEM_LIMIT),
        name="gla",
    )(proj, proj, proj, proj, small, gw, gb, nw, tri, sel, msk)


def _outproj_kernel(x_ref, ya_ref, yb_ref, yc_ref, w_ref, fw_ref, o_ref, *, final):
    y = _dot(ya_ref[...], w_ref[0:SB_WIDTH, :])
    y = y + _dot(yb_ref[...], w_ref[SB_WIDTH:SB_WIDTH + SSD_WIDTH, :])
    y = y + _dot(yc_ref[...], w_ref[SB_WIDTH + SSD_WIDTH:D_INNER, :])
    xn = x_ref[...] + y
    if final:
        ms = jnp.mean(xn * xn, axis=-1, keepdims=True)
        xn = xn * lax.rsqrt(ms + NORM_EPS) * fw_ref[...]
    o_ref[...] = xn


def _outproj(x2, ya, yb, yc, w, fw, *, final, tm=512):
    t = x2.shape[0]
    rows = lambda width: pl.BlockSpec((tm, width), lambda i: (i, 0))
    return pl.pallas_call(
        functools.partial(_outproj_kernel, final=final),
        grid=(t // tm,),
        in_specs=[rows(D_MODEL), rows(SB_WIDTH), rows(SSD_WIDTH), rows(GLA_WIDTH),
                  pl.BlockSpec((D_INNER, D_MODEL), lambda i: (0, 0)),
                  pl.BlockSpec((1, D_MODEL), lambda i: (0, 0))],
        out_specs=rows(D_MODEL),
        out_shape=jax.ShapeDtypeStruct((t, D_MODEL), F32),
        compiler_params=pltpu.CompilerParams(
            dimension_semantics=("arbitrary",), vmem_limit_bytes=VMEM_LIMIT),
        name="outproj_final" if final else "outproj",
    )(x2, ya, yb, yc, w, fw)


def _constants():
    L = CHUNK
    t = np.arange(L)
    tri = (t[:, None] >= t[None, :]).astype(np.float32)
    sel = np.zeros((GLA_LEVELS * L, L), np.float32)
    msk = np.zeros((GLA_LEVELS + 2, L, L), np.float32)
    msk[GLA_LEVELS + 1] = tri
    shift = np.zeros(((SSD_CONV - 1) * L, 2 * L), np.float32)
    for j in range(SSD_CONV - 1):
        shift[j * L + t, L + t - (SSD_CONV - 1 - j)] = 1.0
    for lv in range(GLA_LEVELS):
        mid = ((t >> (lv + 1)) << (lv + 1)) + (1 << lv)
        sel[lv * L + t, mid - 1] = 1.0
        same = (t[:, None] >> (lv + 1)) == (t[None, :] >> (lv + 1))
        up = ((t[:, None] >> lv) & 1) == 1
        low = ((t[None, :] >> lv) & 1) == 0
        msk[lv] = (same & up & low).astype(np.float32)
    msk[GLA_LEVELS] = np.eye(L, dtype=np.float32)
    e16 = np.zeros((LANES, SSD_WIDTH), np.float32)
    for h in range(SSD_HEADS):
        e16[h, h * SSD_HEAD_DIM:(h + 1) * SSD_HEAD_DIM] = 1.0
    return (jnp.asarray(tri, BF16), jnp.asarray(sel, BF16), jnp.asarray(msk, F32), jnp.asarray(e16, BF16),
            jnp.asarray(shift, BF16))


def _pad_lanes(v, offset=0):
    out = jnp.zeros((1, LANES), F32)
    return out.at[0, offset:offset + v.shape[0]].set(v.astype(F32))


def kernel(x, norm_w, w_in, ssd_conv_w, ssd_conv_b, ssd_dt_bias, ssd_a_log, ssd_d, ssd_norm_w,
           gla_gate_w, gla_gate_b, gla_norm_w, w_out, final_norm_w):
    batch, seq, _ = x.shape
    assert seq % 256 == 0 and (batch * seq) % 512 == 0
    tri, sel, msk, e16, shift = _constants()
    x2 = x.reshape(batch * seq, D_MODEL).astype(F32)
    dt_col = OFF_GLA_Q
    glr_col = dt_col + SSD_HEADS + 2 * GLA_KEY + 2 * GLA_WIDTH
    for layer in range(DEPTH):
        w = w_in[layer]
        w_main = jnp.concatenate([w[:, :dt_col], w[:, dt_col + SSD_HEADS:glr_col]], axis=1).astype(BF16)
        w_small = jnp.zeros((D_MODEL, LANES), F32)
        w_small = w_small.at[:, 0:SSD_HEADS].set(w[:, dt_col:dt_col + SSD_HEADS])
        w_small = w_small.at[:, SSD_HEADS:SSD_HEADS + GLA_GATE_RANK].set(w[:, glr_col:glr_col + GLA_GATE_RANK])
        proj, small = _inproj(x2, norm_w[layer].reshape(1, D_MODEL), w_main, w_small.astype(BF16))

        ya = _sb_attention(proj, batch, seq)
        yb = _ssd(proj, small, ssd_conv_w[layer], ssd_conv_b[layer].reshape(1, SSD_CONV_DIM),
                  _pad_lanes(ssd_dt_bias[layer]), _pad_lanes(ssd_a_log[layer]),
                  jnp.repeat(ssd_d[layer], SSD_HEAD_DIM).reshape(1, SSD_WIDTH),
                  ssd_norm_w[layer].reshape(1, SSD_WIDTH), e16, tri, shift, batch, seq)
        gw = jnp.zeros((LANES, GLA_KEY), F32).at[SSD_HEADS:SSD_HEADS + GLA_GATE_RANK, :].set(gla_gate_w[layer])
        yc = _gla(proj, small, gw, gla_gate_b[layer].reshape(1, GLA_KEY),
                  gla_norm_w[layer].reshape(1, GLA_HEAD_V), tri, sel, msk, batch, seq)
        x2 = _outproj(x2, ya, yb, yc, w_out[layer].astype(BF16), final_norm_w.reshape(1, D_MODEL),
                      final=(layer == DEPTH - 1))
    return x2.reshape(batch, seq, D_MODEL).astype(x.dtype)
```

```python
import functools

import jax
import jax.numpy as jnp
import numpy as np
from jax import lax
from jax.experimental import pallas as pl
from jax.experimental.pallas import tpu as pltpu

F32 = jnp.float32
BF16 = jnp.bfloat16

D_MODEL = 1024
DEPTH = 2
D_INNER = 2 * D_MODEL
SB_WIDTH = D_INNER // 4
SB_HEAD_DIM = 64
SSD_WIDTH = D_INNER // 2
SSD_HEAD_DIM = 64
SSD_HEADS = SSD_WIDTH // SSD_HEAD_DIM
SSD_GROUPS = 2
SSD_STATE = 128
SSD_CONV = 4
SSD_CONV_DIM = SSD_WIDTH + 2 * SSD_GROUPS * SSD_STATE
GLA_WIDTH = D_INNER // 4
GLA_HEADS = 4
GLA_KEY = GLA_WIDTH // 2
GLA_HEAD_K = GLA_KEY // GLA_HEADS
GLA_HEAD_V = GLA_WIDTH // GLA_HEADS
GLA_GATE_RANK = 16
GLA_GATE_TAU = 16.0
NORM_EPS = 1e-6

LANES = 128
SUBLANES = 8
CHUNK = 128
SSD_SUBCHUNKS = 2
GLA_SUBCHUNKS = 4
N_MAIN = 4 * SB_WIDTH + SSD_WIDTH + SSD_CONV_DIM + 2 * GLA_KEY + 2 * GLA_WIDTH
OFF_SSD_Z = 4 * SB_WIDTH
OFF_SSD_XBC = OFF_SSD_Z + SSD_WIDTH
OFF_GLA_Q = OFF_SSD_XBC + SSD_CONV_DIM
OFF_GLA_K = OFF_GLA_Q + GLA_KEY
OFF_GLA_V = OFF_GLA_K + GLA_KEY
OFF_GLA_Z = OFF_GLA_V + GLA_WIDTH
GLA_LEVELS = 7
GLA_DIRECT_MAX = 1e30
SB_TQ = 512
LOG2E = 1.4426950408889634
VMEM_LIMIT = 56 * 1024 * 1024


def _dot(a, b):
    return jnp.dot(a, b, preferred_element_type=F32)


def _dot_nt(a, b):
    return lax.dot_general(a, b, (((1,), (1,)), ((), ())), preferred_element_type=F32)


def _dot_tn(a, b):
    return lax.dot_general(a, b, (((0,), (0,)), ((), ())), preferred_element_type=F32)


def _split2(x):
    hi = x.astype(BF16)
    lo = (x - hi.astype(F32)).astype(BF16)
    return hi, lo


def _split3(x):
    hi = x.astype(BF16)
    r = x - hi.astype(F32)
    mid = r.astype(BF16)
    lo = (r - mid.astype(F32)).astype(BF16)
    return hi, mid, lo


def _sigmoid(x):
    return 1.0 / (1.0 + jnp.exp2(x * (-LOG2E)))


def _log_sigmoid(x):
    return jnp.minimum(x, 0.0) - jnp.log(1.0 + jnp.exp(-jnp.abs(x)))


def _softplus(x):
    return jnp.maximum(x, 0.0) + jnp.log(1.0 + jnp.exp(-jnp.abs(x)))


def _inproj_kernel(x_ref, nw_ref, wa_ref, wb_ref, ws_ref, o_ref, os_ref, *, n_chunk):
    x = x_ref[...]
    ms = jnp.mean(x * x, axis=-1, keepdims=True)
    h = (x * lax.rsqrt(ms + NORM_EPS) * nw_ref[...]).astype(BF16)
    for c0 in range(0, OFF_GLA_Q, n_chunk):
        o_ref[:, c0:c0 + n_chunk] = _dot(h, wa_ref[:, c0:c0 + n_chunk]).astype(BF16)
    for c0 in range(0, N_MAIN - OFF_GLA_Q, n_chunk):
        o_ref[:, OFF_GLA_Q + c0:OFF_GLA_Q + c0 + n_chunk] = _dot(h, wb_ref[:, c0:c0 + n_chunk]).astype(BF16)
    os_ref[...] = _dot(h, ws_ref[...])


def _inproj(x2, nw, wa, wb, ws, layer, *, tm=512, n_chunk=1536):
    t = x2.shape[0]
    wspec = lambda cols: pl.BlockSpec((None, D_MODEL, cols), lambda i: (layer, 0, 0))
    return pl.pallas_call(
        functools.partial(_inproj_kernel, n_chunk=n_chunk),
        grid=(t // tm,),
        in_specs=[
            pl.BlockSpec((tm, D_MODEL), lambda i: (i, 0)),
            pl.BlockSpec((1, D_MODEL), lambda i: (0, 0)),
            wspec(OFF_GLA_Q), wspec(N_MAIN - OFF_GLA_Q), wspec(LANES),
        ],
        out_specs=[
            pl.BlockSpec((tm, N_MAIN), lambda i: (i, 0)),
            pl.BlockSpec((tm, LANES), lambda i: (i, 0)),
        ],
        out_shape=[
            jax.ShapeDtypeStruct((t, N_MAIN), BF16),
            jax.ShapeDtypeStruct((t, LANES), F32),
        ],
        compiler_params=pltpu.CompilerParams(
            dimension_semantics=("arbitrary",), vmem_limit_bytes=VMEM_LIMIT),
        name="inproj",
    )(x2, nw, wa, wb, ws)


def _sb_kernel(q_ref, k_ref, v_ref, z_ref, o_ref, k0_s, k1_s, vt_s, acc_s, zt_s, a_s, qs_s, *, seq):
    kb_n = LANES
    n_kb = seq // kb_n
    groups = kb_n // SUBLANES
    hd = SB_HEAD_DIM
    tq = SB_TQ
    U = tq // kb_n

    lane = lax.broadcasted_iota(jnp.int32, (kb_n, kb_n), 1)
    row = lax.broadcasted_iota(jnp.int32, (kb_n, kb_n), 0)
    src = (row % SUBLANES) * groups + row // SUBLANES
    perm = jnp.where(lane == src, 1.0, 0.0).astype(BF16)
    src_t = (lane % SUBLANES) * groups + lane // SUBLANES
    perm_t = jnp.where(row == src_t, 1.0, 0.0).astype(BF16)
    lo_lanes = lane < hd

    def prep(n, carry):
        for j in range(U):
            kb = n * U + j
            r0 = pl.multiple_of(kb * kb_n, kb_n)
            kp = _dot(perm, k_ref[pl.ds(r0, kb_n), :]).astype(BF16)
            k0_s[kb] = jnp.where(lo_lanes, kp, jnp.zeros_like(kp))
            k1_s[kb] = jnp.where(lo_lanes, jnp.zeros_like(kp), kp)
            vt = v_ref[pl.ds(r0, kb_n), :].astype(F32).T.astype(BF16)
            vt_s[kb] = _dot(vt, perm_t).astype(BF16)
            qs_s[pl.ds(r0, kb_n), :] = (q_ref[pl.ds(r0, kb_n), :].astype(F32) * (0.5 * hd ** -0.5)).astype(BF16)
        return carry

    lax.fori_loop(0, n_kb // U, prep, 0)

    def block_scan(zt, base):
        nl = zt.shape[1]
        seg = lax.broadcasted_iota(jnp.int32, (SUBLANES, nl), 0)
        run = jnp.ones((SUBLANES, nl), F32)
        s_rows = [None] * groups
        for i in reversed(range(groups)):
            zh = zt[i * SUBLANES:(i + 1) * SUBLANES, :]
            beta = 0.5 * jnp.tanh(zh) + 0.5
            if base is not None:
                beta = jnp.where(i < base, beta, 0.0)
            s_rows[i] = beta * run
            run = run - s_rows[i]
        inc = run
        for sh in (1, 2, 4):
            inc = inc * jnp.where(seg < SUBLANES - sh, pltpu.roll(inc, SUBLANES - sh, 0), 1.0)
        excl = jnp.where(seg < SUBLANES - 1, pltpu.roll(inc, SUBLANES - 1, 0), 1.0)
        total = jnp.broadcast_to(inc[0:1, :], (SUBLANES, nl))
        return s_rows, excl, total

    def chunk(ztc, base, cc):
        s_rows, excl, total = block_scan(ztc, base)
        off = excl * cc
        a = jnp.concatenate([s_rows[i] * off for i in range(groups)], axis=0)
        return a, cc * total

    seg_d = lax.broadcasted_iota(jnp.int32, (SUBLANES, kb_n), 0)
    lane_d = lax.broadcasted_iota(jnp.int32, (SUBLANES, kb_n), 1)
    base_d = lane_d - seg_d * groups

    n_q = seq // tq

    def issue_scores(kb0, qi, slot):
        qb = qs_s[qi * tq:(qi + 1) * tq, :]
        k_cat = jnp.concatenate([k0_s[kb0 + j] for j in range(U)] + [k1_s[kb0 + j] for j in range(U)], axis=0)
        zt_s[slot] = _dot_nt(k_cat, qb)

    def add_values(kb0, slot):
        vts = [vt_s[kb0 + j] for j in range(U)]
        for h in range(2):
            v_cat = jnp.concatenate([vts[j][h * hd:(h + 1) * hd, :] for j in range(U)], axis=1)
            acc_s[h] += _dot(v_cat, a_s[slot, h])

    def finish_qblock(qi):
        q0 = qi * tq
        o = jnp.concatenate([acc_s[0], acc_s[1]], axis=0).T
        zg = z_ref[pl.ds(q0, tq), :].astype(F32)
        o_ref[pl.ds(q0, tq), :] = (o * (zg * _sigmoid(zg))).astype(BF16)
        acc_s[...] = jnp.zeros_like(acc_s)

    def weights(slot, carries, diag):
        out = []
        for h in range(2):
            carry = carries[h]
            for j in reversed(range(U)):
                r0 = (h * U + j) * kb_n
                if not diag:
                    a, carry = chunk(zt_s[slot, r0:r0 + kb_n, :], None, carry)
                else:
                    l0 = j * kb_n
                    parts, cparts = [], []
                    if l0 > 0:
                        parts.append(jnp.zeros((kb_n, l0), F32))
                        cparts.append(carry[:, 0:l0])
                    a_m, c_m = chunk(zt_s[slot, r0:r0 + kb_n, l0:l0 + kb_n], base_d, carry[:, l0:l0 + kb_n])
                    parts.append(a_m)
                    cparts.append(c_m)
                    if l0 + kb_n < tq:
                        a_u, c_u = chunk(zt_s[slot, r0:r0 + kb_n, l0 + kb_n:tq], None, carry[:, l0 + kb_n:tq])
                        parts.append(a_u)
                        cparts.append(c_u)
                    a = jnp.concatenate(parts, axis=1)
                    carry = jnp.concatenate(cparts, axis=1)
                a_s[slot, h, j * kb_n:(j + 1) * kb_n, :] = a.astype(BF16)
            out.append(carry)
        return tuple(out)

    items = []
    for qi in range(n_q):
        items.append((qi, qi * U, True))
        items.extend((qi, j * U, False) for j in reversed(range(qi)))

    acc_s[...] = jnp.zeros_like(acc_s)
    issue_scores(items[0][1], items[0][0], 0)
    carries = None
    for n, (qi, kb0, diag) in enumerate(items):
        if n + 1 < len(items):
            issue_scores(items[n + 1][1], items[n + 1][0], (n + 1) % 2)
        if n > 0:
            add_values(items[n - 1][1], (n - 1) % 2)
        if diag:
            if qi > 0:
                finish_qblock(qi - 1)
            one = jnp.ones((SUBLANES, tq), F32)
            carries = (one, one)
        carries = weights(n % 2, carries, diag)
    add_values(items[-1][1], (len(items) - 1) % 2)
    finish_qblock(n_q - 1)


def _sb_attention(proj, batch, seq):
    pairs = SB_WIDTH // LANES
    n_kb = seq // LANES
    blk = lambda off: pl.BlockSpec((seq, LANES), lambda b, p, off=off: (b, off + p))
    return pl.pallas_call(
        functools.partial(_sb_kernel, seq=seq),
        grid=(batch, pairs),
        in_specs=[blk(0), blk(pairs), blk(2 * pairs), blk(3 * pairs)],
        out_specs=pl.BlockSpec((seq, LANES), lambda b, p: (b, p)),
        out_shape=jax.ShapeDtypeStruct((batch * seq, SB_WIDTH), BF16),
        scratch_shapes=[
            pltpu.VMEM((n_kb, LANES, LANES), BF16),
            pltpu.VMEM((n_kb, LANES, LANES), BF16),
            pltpu.VMEM((n_kb, LANES, LANES), BF16),
            pltpu.VMEM((2, SB_HEAD_DIM, SB_TQ), F32),
            pltpu.VMEM((2, 2 * SB_TQ, SB_TQ), F32),
            pltpu.VMEM((2, 2, SB_TQ, SB_TQ), BF16),
            pltpu.VMEM((seq, LANES), BF16),
        ],
        compiler_params=pltpu.CompilerParams(
            dimension_semantics=("arbitrary", "arbitrary"), vmem_limit_bytes=VMEM_LIMIT),
        name="sb_attention",
    )(proj, proj, proj, proj)


def _ssd_kernel(z_ref, xbc_ref, xprev_ref, sm_ref, cw_ref, cb_ref, dtb_ref, alog_ref, dexp_ref, nw_ref,
                e16_ref, tri_ref, shift_ref, o_ref, h_s):
    first = pl.program_id(1) == 0

    @pl.when(first)
    def _():
        h_s[...] = jnp.zeros_like(h_s)

    for sub in range(SSD_SUBCHUNKS):
        _ssd_chunk(sub, first, z_ref, xbc_ref, xprev_ref, sm_ref, cw_ref, cb_ref, dtb_ref, alog_ref,
                   dexp_ref, nw_ref, e16_ref, tri_ref, shift_ref, o_ref, h_s)


def _ssd_chunk(sub, first, z_ref, xbc_ref, xprev_ref, sm_ref, cw_ref, cb_ref, dtb_ref, alog_ref,
               dexp_ref, nw_ref, e16_ref, tri_ref, shift_ref, o_ref, h_s):
    L = CHUNK
    gw = SSD_WIDTH // SSD_GROUPS
    hpg = SSD_HEADS // SSD_GROUPS
    n = SSD_STATE
    rows = slice(sub * L, (sub + 1) * L)

    cur = xbc_ref[rows, :]
    if sub == 0:
        prev = xprev_ref[...]
        prev = jnp.where(first, jnp.zeros_like(prev), prev)
    else:
        prev = xbc_ref[(sub - 1) * L:sub * L, :]
    both = jnp.concatenate([prev, cur], axis=0)
    shifted = _dot(shift_ref[...], both)
    acc = cb_ref[...] + cw_ref[SSD_CONV - 1:SSD_CONV, :] * cur.astype(F32)
    for j in range(SSD_CONV - 1):
        acc = acc + cw_ref[j:j + 1, :] * shifted[j * L:(j + 1) * L, :]
    xbc = acc * _sigmoid(acc)
    xs = xbc[:, 0:SSD_WIDTH]

    lane = lax.broadcasted_iota(jnp.int32, (L, LANES), 1)
    head_lanes = lane < SSD_HEADS
    dt = jnp.where(head_lanes, _softplus(sm_ref[rows, :] + dtb_ref[...]), 0.0)
    a = dt * (-jnp.exp(alog_ref[...]))
    tri = tri_ref[...]
    a1, a2, a3 = _split3(a)
    cs = _dot(tri, a1) + _dot(tri, a2) + _dot(tri, a3)
    cs_t = cs.T
    ecs = jnp.exp(cs)
    dend = jnp.exp(cs[L - 1:L, :] - cs)
    stack = jnp.concatenate([dt, dt * dend, ecs], axis=0)
    s_hi, s_lo = _split2(stack)
    e16 = e16_ref[...]
    ex = _dot(s_hi, e16) + _dot(s_lo, e16)
    x_dt = (xs * ex[0:L, :]).astype(BF16)
    x_st = (xs * ex[L:2 * L, :]).astype(BF16)
    ecs_x = ex[2 * L:3 * L, :]

    rowi = lax.broadcasted_iota(jnp.int32, (L, L), 0)
    coli = lax.broadcasted_iota(jnp.int32, (L, L), 1)
    causal = rowi >= coli
    lo_lanes = coli < SSD_HEAD_DIM

    y_parts = []
    for g in range(SSD_GROUPS):
        bm = xbc[:, SSD_WIDTH + g * n:SSD_WIDTH + (g + 1) * n].astype(BF16)
        cm = xbc[:, SSD_WIDTH + SSD_GROUPS * n + g * n:SSD_WIDTH + SSD_GROUPS * n + (g + 1) * n].astype(BF16)
        scores = _dot_nt(cm, bm)
        for pr in range(hpg // 2):
            col = g * gw + pr * LANES
            xp = x_dt[:, col:col + LANES]
            yd = None
            for half in range(2):
                h = g * hpg + 2 * pr + half
                sg = cs[:, h:h + 1] - cs_t[h:h + 1, :]
                dec = jnp.where(causal, jnp.exp(jnp.where(causal, sg, 0.0)), 0.0)
                w = (scores * dec).astype(BF16)
                xm = jnp.where(lo_lanes, xp, jnp.zeros_like(xp)) if half == 0 else jnp.where(
                    lo_lanes, jnp.zeros_like(xp), xp)
                t = _dot(w, xm)
                yd = t if yd is None else yd + t
            y_parts.append(yd)
        h_prev = h_s[g]
        y_off = _dot(cm, h_prev.astype(BF16)) * ecs_x[:, g * gw:(g + 1) * gw]
        y_parts.append(y_off)
        st_new = _dot_tn(bm, x_st[:, g * gw:(g + 1) * gw])
        h_s[g] = h_prev * ecs_x[L - 1:L, g * gw:(g + 1) * gw] + st_new

    per_g = 1 + hpg // 2
    outs = []
    zg = z_ref[rows, :].astype(F32)
    for g in range(SSD_GROUPS):
        yd = jnp.concatenate(y_parts[g * per_g:g * per_g + hpg // 2], axis=1)
        y = yd + y_parts[g * per_g + hpg // 2] + dexp_ref[:, g * gw:(g + 1) * gw] * xs[:, g * gw:(g + 1) * gw]
        zz = zg[:, g * gw:(g + 1) * gw]
        y = y * (zz * _sigmoid(zz))
        ms = jnp.mean(y * y, axis=-1, keepdims=True)
        outs.append(y * lax.rsqrt(ms + NORM_EPS) * nw_ref[:, g * gw:(g + 1) * gw])
    o_ref[rows, :] = jnp.concatenate(outs, axis=1).astype(BF16)


def _ssd(proj, small, cw, cb, dtb, alog, dexp, nw, e16, tri, shift, batch, seq):
    step = SSD_SUBCHUNKS * CHUNK
    nc = seq // step
    row = lambda b, c: b * nc + c
    prev_row = lambda b, c: b * (seq // CHUNK) + jnp.maximum(c * SSD_SUBCHUNKS - 1, 0)
    const = lambda shape: pl.BlockSpec(shape, lambda b, c: (0,) * len(shape))
    return pl.pallas_call(
        _ssd_kernel,
        grid=(batch, nc),
        in_specs=[
            pl.BlockSpec((step, SSD_WIDTH), lambda b, c: (row(b, c), OFF_SSD_Z // SSD_WIDTH)),
            pl.BlockSpec((step, SSD_CONV_DIM), lambda b, c: (row(b, c), OFF_SSD_XBC // SSD_CONV_DIM)),
            pl.BlockSpec((CHUNK, SSD_CONV_DIM), lambda b, c: (prev_row(b, c), OFF_SSD_XBC // SSD_CONV_DIM)),
            pl.BlockSpec((step, LANES), lambda b, c: (row(b, c), 0)),
            const((SSD_CONV, SSD_CONV_DIM)), const((1, SSD_CONV_DIM)), const((1, LANES)), const((1, LANES)),
            const((1, SSD_WIDTH)), const((1, SSD_WIDTH)), const((LANES, SSD_WIDTH)), const((CHUNK, CHUNK)),
            const(((SSD_CONV - 1) * CHUNK, 2 * CHUNK)),
        ],
        out_specs=pl.BlockSpec((step, SSD_WIDTH), lambda b, c: (row(b, c), 0)),
        out_shape=jax.ShapeDtypeStruct((batch * seq, SSD_WIDTH), BF16),
        scratch_shapes=[
            pltpu.VMEM((SSD_GROUPS, SSD_STATE, SSD_WIDTH // SSD_GROUPS), F32),
        ],
        compiler_params=pltpu.CompilerParams(
            dimension_semantics=("arbitrary", "arbitrary"), vmem_limit_bytes=VMEM_LIMIT),
        name="ssd",
    )(proj, proj, proj, small, cw, cb, dtb, alog, dexp, nw, e16, tri, shift)


def _gla_kernel(q_ref, k_ref, v_ref, z_ref, sm_ref, gw_ref, gb_ref, nw_ref, tri_ref, sel_ref,
                msk_ref, o_ref, st_s, att_s):
    L = CHUNK
    kk = GLA_HEAD_K
    vv = GLA_HEAD_V

    @pl.when(pl.program_id(1) == 0)
    def _():
        st_s[...] = jnp.zeros_like(st_s)

    lane = lax.broadcasted_iota(jnp.int32, (L, GLA_KEY), 1)
    rowi = lax.broadcasted_iota(jnp.int32, (L, GLA_KEY), 0)
    head_of_lane = lane // kk

    def head_mask(xb, h):
        return jnp.where(head_of_lane == h, xb, jnp.zeros_like(xb))

    g_hi, g_lo = _split2(gw_ref[...])
    tri = tri_ref[...]
    subs = []
    for sub in range(GLA_SUBCHUNKS):
        rows = slice(sub * L, (sub + 1) * L)
        s_hi, s_lo = _split2(sm_ref[rows, :])
        u = _dot(s_hi, g_hi) + _dot(s_lo, g_hi) + _dot(s_hi, g_lo) + gb_ref[...]
        lg = _log_sigmoid(u) * (1.0 / GLA_GATE_TAU)
        l1, l2, l3 = _split3(lg)
        gc = _dot(tri, l1) + _dot(tri, l2) + _dot(tri, l3)
        q = q_ref[rows, :].astype(F32) * (kk ** -0.5)
        k = k_ref[rows, :].astype(F32)
        g_last = gc[L - 1:L, :]
        q_in = (q * jnp.exp(gc)).astype(BF16)
        k_dec = (k * jnp.exp(g_last - gc)).astype(BF16)
        k_growth = jnp.max(jnp.abs(k), axis=0, keepdims=True) * jnp.exp(-g_last)
        subs.append(dict(rows=rows, gc=gc, q=q, k=k, g_last=g_last, q_in=q_in, k_dec=k_dec, growth=k_growth))

    growth = subs[0]["growth"]
    for s in subs[1:]:
        growth = jnp.maximum(growth, s["growth"])
    direct = jnp.max(growth) < GLA_DIRECT_MAX

    @pl.when(direct)
    def _():
        m = msk_ref[GLA_LEVELS + 1] > 0.0
        for i, s in enumerate(subs):
            k_up = (s["k"] * jnp.exp(-s["gc"])).astype(BF16)
            for h in range(GLA_HEADS):
                att_s[i, h] = jnp.where(m, _dot_nt(head_mask(s["q_in"], h), k_up), 0.0)

    @pl.when(jnp.logical_not(direct))
    def _():
        sel = sel_ref[...]
        for i, s in enumerate(subs):
            gc, q, k = s["gc"], s["q"], s["k"]
            c1, c2, c3 = _split3(gc)
            ref_g = _dot(sel, c1) + _dot(sel, c2) + _dot(sel, c3)
            att = [None] * GLA_HEADS
            for lv in range(GLA_LEVELS):
                expo = -jnp.abs(gc - ref_g[lv * L:(lv + 1) * L, :])
                upper = ((rowi >> lv) & 1) == 1
                xb = (jnp.where(upper, q, k) * jnp.exp(expo)).astype(BF16)
                m = msk_ref[lv]
                for h in range(GLA_HEADS):
                    p = _dot_nt(head_mask(xb, h), xb) * m
                    att[h] = p if att[h] is None else att[h] + p
            qb = q.astype(BF16)
            kb = k.astype(BF16)
            m = msk_ref[GLA_LEVELS]
            for h in range(GLA_HEADS):
                att_s[i, h] = att[h] + _dot_nt(head_mask(qb, h), kb) * m

    lane_s = lax.broadcasted_iota(jnp.int32, (vv, GLA_KEY), 1) // kk
    st = st_s[...]
    for i, s in enumerate(subs):
        rows = s["rows"]
        st_b = st.astype(BF16)
        st_new = st * jnp.exp(s["g_last"])
        outs = []
        for h in range(GLA_HEADS):
            vh = v_ref[rows, h * vv:(h + 1) * vv]
            o = _dot_nt(head_mask(s["q_in"], h), st_b) + _dot(att_s[i, h].astype(BF16), vh)
            upd = _dot_tn(vh, s["k_dec"])
            st_new = st_new + jnp.where(lane_s == h, upd, 0.0)
            ms = jnp.mean(o * o, axis=-1, keepdims=True)
            o = o * lax.rsqrt(ms + NORM_EPS) * nw_ref[...]
            zz = z_ref[rows, h * vv:(h + 1) * vv].astype(F32)
            outs.append(o * (zz * _sigmoid(zz)))
        o_ref[rows, :] = jnp.concatenate(outs, axis=1).astype(BF16)
        st = st_new
    st_s[...] = st


def _gla(proj, small, gw, gb, nw, tri, sel, msk, batch, seq):
    step = GLA_SUBCHUNKS * CHUNK
    nc = seq // step
    row = lambda b, c: b * nc + c
    const = lambda shape: pl.BlockSpec(shape, lambda b, c: (0,) * len(shape))
    return pl.pallas_call(
        _gla_kernel,
        grid=(batch, nc),
        in_specs=[
            pl.BlockSpec((step, GLA_KEY), lambda b, c: (row(b, c), OFF_GLA_Q // GLA_KEY)),
            pl.BlockSpec((step, GLA_KEY), lambda b, c: (row(b, c), OFF_GLA_K // GLA_KEY)),
            pl.BlockSpec((step, GLA_WIDTH), lambda b, c: (row(b, c), OFF_GLA_V // GLA_WIDTH)),
            pl.BlockSpec((step, GLA_WIDTH), lambda b, c: (row(b, c), OFF_GLA_Z // GLA_WIDTH)),
            pl.BlockSpec((step, LANES), lambda b, c: (row(b, c), 0)),
            const((LANES, GLA_KEY)), const((1, GLA_KEY)), const((1, GLA_HEAD_V)),
            const((CHUNK, CHUNK)), const((GLA_LEVELS * CHUNK, CHUNK)), const((GLA_LEVELS + 2, CHUNK, CHUNK)),
        ],
        out_specs=pl.BlockSpec((step, GLA_WIDTH), lambda b, c: (row(b, c), 0)),
        out_shape=jax.ShapeDtypeStruct((batch * seq, GLA_WIDTH), BF16),
        scratch_shapes=[pltpu.VMEM((GLA_HEAD_V, GLA_KEY), F32),
                        pltpu.VMEM((GLA_SUBCHUNKS, GLA_HEADS, CHUNK, CHUNK), F32)],
        compiler_params=pltpu.CompilerParams(
            dimension_semantics=("arbitrary", "arbitrary"), vmem_limit_bytes=VMEM_LIMIT),
        name="gla",
    )(proj, proj, proj, proj, small, gw, gb, nw, tri, sel, msk)


def _outproj_kernel(x_ref, ya_ref, yb_ref, yc_ref, w_ref, fw_ref, o_ref, *, final):
    y = _dot(ya_ref[...], w_ref[0:SB_WIDTH, :])
    y = y + _dot(yb_ref[...], w_ref[SB_WIDTH:SB_WIDTH + SSD_WIDTH, :])
    y = y + _dot(yc_ref[...], w_ref[SB_WIDTH + SSD_WIDTH:D_INNER, :])
    xn = x_ref[...] + y
    if final:
        ms = jnp.mean(xn * xn, axis=-1, keepdims=True)
        xn = xn * lax.rsqrt(ms + NORM_EPS) * fw_ref[...]
    o_ref[...] = xn


def _outproj(x2, ya, yb, yc, w, fw, layer, *, final, tm=512):
    t = x2.shape[0]
    rows = lambda width: pl.BlockSpec((tm, width), lambda i: (i, 0))
    return pl.pallas_call(
        functools.partial(_outproj_kernel, final=final),
        grid=(t // tm,),
        in_specs=[rows(D_MODEL), rows(SB_WIDTH), rows(SSD_WIDTH), rows(GLA_WIDTH),
                  pl.BlockSpec((None, D_INNER, D_MODEL), lambda i: (layer, 0, 0)),
                  pl.BlockSpec((1, D_MODEL), lambda i: (0, 0))],
        out_specs=rows(D_MODEL),
        out_shape=jax.ShapeDtypeStruct((t, D_MODEL), F32),
        compiler_params=pltpu.CompilerParams(
            dimension_semantics=("arbitrary",), vmem_limit_bytes=VMEM_LIMIT),
        name="outproj_final" if final else "outproj",
    )(x2, ya, yb, yc, w, fw)


def _constants():
    L = CHUNK
    t = np.arange(L)
    tri = (t[:, None] >= t[None, :]).astype(np.float32)
    sel = np.zeros((GLA_LEVELS * L, L), np.float32)
    msk = np.zeros((GLA_LEVELS + 2, L, L), np.float32)
    msk[GLA_LEVELS + 1] = tri
    shift = np.zeros(((SSD_CONV - 1) * L, 2 * L), np.float32)
    for j in range(SSD_CONV - 1):
        shift[j * L + t, L + t - (SSD_CONV - 1 - j)] = 1.0
    for lv in range(GLA_LEVELS):
        mid = ((t >> (lv + 1)) << (lv + 1)) + (1 << lv)
        sel[lv * L + t, mid - 1] = 1.0
        same = (t[:, None] >> (lv + 1)) == (t[None, :] >> (lv + 1))
        up = ((t[:, None] >> lv) & 1) == 1
        low = ((t[None, :] >> lv) & 1) == 0
        msk[lv] = (same & up & low).astype(np.float32)
    msk[GLA_LEVELS] = np.eye(L, dtype=np.float32)
    e16 = np.zeros((LANES, SSD_WIDTH), np.float32)
    for h in range(SSD_HEADS):
        e16[h, h * SSD_HEAD_DIM:(h + 1) * SSD_HEAD_DIM] = 1.0
    return (jnp.asarray(tri, BF16), jnp.asarray(sel, BF16), jnp.asarray(msk, F32), jnp.asarray(e16, BF16),
            jnp.asarray(shift, BF16))


def _pad_lanes(v, offset=0):
    out = jnp.zeros((1, LANES), F32)
    return out.at[0, offset:offset + v.shape[0]].set(v.astype(F32))


def kernel(x, norm_w, w_in, ssd_conv_w, ssd_conv_b, ssd_dt_bias, ssd_a_log, ssd_d, ssd_norm_w,
           gla_gate_w, gla_gate_b, gla_norm_w, w_out, final_norm_w):
    batch, seq, _ = x.shape
    assert seq % 256 == 0 and (batch * seq) % 512 == 0
    tri, sel, msk, e16, shift = _constants()
    x2 = x.reshape(batch * seq, D_MODEL).astype(F32)
    dt_col = OFF_GLA_Q
    glr_col = dt_col + SSD_HEADS + 2 * GLA_KEY + 2 * GLA_WIDTH
    wa = w_in[:, :, :dt_col].astype(BF16)
    wb = w_in[:, :, dt_col + SSD_HEADS:glr_col].astype(BF16)
    ws = jnp.concatenate(
        [w_in[:, :, dt_col:dt_col + SSD_HEADS], w_in[:, :, glr_col:glr_col + GLA_GATE_RANK],
         jnp.zeros((DEPTH, D_MODEL, LANES - SSD_HEADS - GLA_GATE_RANK), w_in.dtype)], axis=2).astype(BF16)
    w_out_b = w_out.astype(BF16)
    for layer in range(DEPTH):
        proj, small = _inproj(x2, norm_w[layer].reshape(1, D_MODEL), wa, wb, ws, layer)

        ya = _sb_attention(proj, batch, seq)
        yb = _ssd(proj, small, ssd_conv_w[layer], ssd_conv_b[layer].reshape(1, SSD_CONV_DIM),
                  _pad_lanes(ssd_dt_bias[layer]), _pad_lanes(ssd_a_log[layer]),
                  jnp.repeat(ssd_d[layer], SSD_HEAD_DIM).reshape(1, SSD_WIDTH),
                  ssd_norm_w[layer].reshape(1, SSD_WIDTH), e16, tri, shift, batch, seq)
        gw = jnp.zeros((LANES, GLA_KEY), F32).at[SSD_HEADS:SSD_HEADS + GLA_GATE_RANK, :].set(gla_gate_w[layer])
        yc = _gla(proj, small, gw, gla_gate_b[layer].reshape(1, GLA_KEY),
                  gla_norm_w[layer].reshape(1, GLA_HEAD_V), tri, sel, msk, batch, seq)
        x2 = _outproj(x2, ya, yb, yc, w_out_b, final_norm_w.reshape(1, D_MODEL), layer,
                      final=(layer == DEPTH - 1))
    return x2.reshape(batch, seq, D_MODEL).astype(x.dtype)
```

```python
import functools

import jax
import jax.numpy as jnp
import numpy as np
from jax import lax
from jax.experimental import pallas as pl
from jax.experimental.pallas import tpu as pltpu

F32 = jnp.float32
BF16 = jnp.bfloat16

D_MODEL = 1024
DEPTH = 2
D_INNER = 2 * D_MODEL
SB_WIDTH = D_INNER // 4
SB_HEAD_DIM = 64
SSD_WIDTH = D_INNER // 2
SSD_HEAD_DIM = 64
SSD_HEADS = SSD_WIDTH // SSD_HEAD_DIM
SSD_GROUPS = 2
SSD_STATE = 128
SSD_CONV = 4
SSD_CONV_DIM = SSD_WIDTH + 2 * SSD_GROUPS * SSD_STATE
GLA_WIDTH = D_INNER // 4
GLA_HEADS = 4
GLA_KEY = GLA_WIDTH // 2
GLA_HEAD_K = GLA_KEY // GLA_HEADS
GLA_HEAD_V = GLA_WIDTH // GLA_HEADS
GLA_GATE_RANK = 16
GLA_GATE_TAU = 16.0
NORM_EPS = 1e-6

LANES = 128
SUBLANES = 8
CHUNK = 128
MIX_SUBCHUNKS = 4
N_MAIN = 4 * SB_WIDTH + SSD_WIDTH + SSD_CONV_DIM + 2 * GLA_KEY + 2 * GLA_WIDTH
OFF_SSD_Z = 4 * SB_WIDTH
OFF_SSD_XBC = OFF_SSD_Z + SSD_WIDTH
OFF_GLA_Q = OFF_SSD_XBC + SSD_CONV_DIM
OFF_GLA_K = OFF_GLA_Q + GLA_KEY
OFF_GLA_V = OFF_GLA_K + GLA_KEY
OFF_GLA_Z = OFF_GLA_V + GLA_WIDTH
GLA_LEVELS = 7
GLA_DIRECT_MAX = 1e30
SB_TQ = 512
LOG2E = 1.4426950408889634
VMEM_LIMIT = 56 * 1024 * 1024


def _dot(a, b):
    return jnp.dot(a, b, preferred_element_type=F32)


def _dot_nt(a, b):
    return lax.dot_general(a, b, (((1,), (1,)), ((), ())), preferred_element_type=F32)


def _dot_tn(a, b):
    return lax.dot_general(a, b, (((0,), (0,)), ((), ())), preferred_element_type=F32)


def _split2(x):
    hi = x.astype(BF16)
    lo = (x - hi.astype(F32)).astype(BF16)
    return hi, lo


def _split3(x):
    hi = x.astype(BF16)
    r = x - hi.astype(F32)
    mid = r.astype(BF16)
    lo = (r - mid.astype(F32)).astype(BF16)
    return hi, mid, lo


def _silu(x):
    hx = 0.5 * x
    return hx * jnp.tanh(hx) + hx


def _log_sigmoid(x):
    return jnp.minimum(x, 0.0) - jnp.log(1.0 + jnp.exp(-jnp.abs(x)))


def _softplus(x):
    return jnp.maximum(x, 0.0) + jnp.log(1.0 + jnp.exp(-jnp.abs(x)))


def _inproj_kernel(x_ref, nw_ref, wa_ref, wb_ref, ws_ref, o_ref, os_ref, *, n_chunk):
    x = x_ref[...]
    ms = jnp.mean(x * x, axis=-1, keepdims=True)
    h = (x * lax.rsqrt(ms + NORM_EPS) * nw_ref[...]).astype(BF16)
    for c0 in range(0, OFF_GLA_Q, n_chunk):
        o_ref[:, c0:c0 + n_chunk] = _dot(h, wa_ref[:, c0:c0 + n_chunk]).astype(BF16)
    for c0 in range(0, N_MAIN - OFF_GLA_Q, n_chunk):
        o_ref[:, OFF_GLA_Q + c0:OFF_GLA_Q + c0 + n_chunk] = _dot(h, wb_ref[:, c0:c0 + n_chunk]).astype(BF16)
    os_ref[...] = _dot(h, ws_ref[...])


def _inproj(x2, nw, wa, wb, ws, *, tm=512, n_chunk=1536):
    t = x2.shape[0]
    wspec = lambda cols: pl.BlockSpec((D_MODEL, cols), lambda i: (0, 0))
    return pl.pallas_call(
        functools.partial(_inproj_kernel, n_chunk=n_chunk),
        grid=(t // tm,),
        in_specs=[
            pl.BlockSpec((tm, D_MODEL), lambda i: (i, 0)),
            pl.BlockSpec((1, D_MODEL), lambda i: (0, 0)),
            wspec(OFF_GLA_Q), wspec(N_MAIN - OFF_GLA_Q), wspec(LANES),
        ],
        out_specs=[
            pl.BlockSpec((tm, N_MAIN), lambda i: (i, 0)),
            pl.BlockSpec((tm, LANES), lambda i: (i, 0)),
        ],
        out_shape=[
            jax.ShapeDtypeStruct((t, N_MAIN), BF16),
            jax.ShapeDtypeStruct((t, LANES), F32),
        ],
        compiler_params=pltpu.CompilerParams(
            dimension_semantics=("arbitrary",), vmem_limit_bytes=VMEM_LIMIT),
        name="inproj",
    )(x2, nw, wa, wb, ws)


def _sb_kernel(q_ref, k_ref, v_ref, z_ref, o_ref, k0_s, k1_s, vt_s, acc_s, zt0_s, zt1_s, a0_s, a1_s, qs_s, *, seq):
    zt_s = (zt0_s, zt1_s)
    a_s = (a0_s, a1_s)
    kb_n = LANES
    n_kb = seq // kb_n
    groups = kb_n // SUBLANES
    hd = SB_HEAD_DIM
    tq = SB_TQ
    U = tq // kb_n

    lane = lax.broadcasted_iota(jnp.int32, (kb_n, kb_n), 1)
    row = lax.broadcasted_iota(jnp.int32, (kb_n, kb_n), 0)
    src = (row % SUBLANES) * groups + row // SUBLANES
    perm = jnp.where(lane == src, 1.0, 0.0).astype(BF16)
    src_t = (lane % SUBLANES) * groups + lane // SUBLANES
    perm_t = jnp.where(row == src_t, 1.0, 0.0).astype(BF16)
    lo_lanes = lane < hd

    def prep(n, carry):
        for j in range(U):
            kb = n * U + j
            r0 = pl.multiple_of(kb * kb_n, kb_n)
            kp = _dot(perm, k_ref[pl.ds(r0, kb_n), :]).astype(BF16)
            k0_s[kb] = jnp.where(lo_lanes, kp, jnp.zeros_like(kp))
            k1_s[kb] = jnp.where(lo_lanes, jnp.zeros_like(kp), kp)
            vt = v_ref[pl.ds(r0, kb_n), :].astype(F32).T.astype(BF16)
            vt_s[kb] = _dot(vt, perm_t).astype(BF16)
            qs_s[pl.ds(r0, kb_n), :] = (q_ref[pl.ds(r0, kb_n), :].astype(F32) * (0.5 * hd ** -0.5)).astype(BF16)
        return carry

    lax.fori_loop(0, n_kb // U, prep, 0)

    def block_scan(zt, base):
        nl = zt.shape[1]
        seg = lax.broadcasted_iota(jnp.int32, (SUBLANES, nl), 0)
        run = jnp.ones((SUBLANES, nl), F32)
        s_rows = [None] * groups
        for i in reversed(range(groups)):
            zh = zt[i * SUBLANES:(i + 1) * SUBLANES, :]
            beta = 0.5 * jnp.tanh(zh) + 0.5
            if base is not None:
                beta = jnp.where(i < base, beta, 0.0)
            s_rows[i] = beta * run
            run = run - s_rows[i]
        inc = run
        for sh in (1, 2, 4):
            inc = inc * jnp.where(seg < SUBLANES - sh, pltpu.roll(inc, SUBLANES - sh, 0), 1.0)
        excl = jnp.where(seg < SUBLANES - 1, pltpu.roll(inc, SUBLANES - 1, 0), 1.0)
        total = jnp.broadcast_to(inc[0:1, :], (SUBLANES, nl))
        return s_rows, excl, total

    def chunk(ztc, base, cc):
        s_rows, excl, total = block_scan(ztc, base)
        off = excl * cc
        a = jnp.concatenate([s_rows[i] * off for i in range(groups)], axis=0)
        return a, cc * total

    seg_d = lax.broadcasted_iota(jnp.int32, (SUBLANES, kb_n), 0)
    lane_d = lax.broadcasted_iota(jnp.int32, (SUBLANES, kb_n), 1)
    base_d = lane_d - seg_d * groups

    n_q = seq // tq

    def issue_scores(kb0, qi, slot):
        qb = qs_s[qi * tq:(qi + 1) * tq, :]
        k_cat = jnp.concatenate([k0_s[kb0 + j] for j in range(U)] + [k1_s[kb0 + j] for j in range(U)], axis=0)
        zt_s[slot][...] = _dot_nt(k_cat, qb)

    def add_values(kb0, slot):
        vts = [vt_s[kb0 + j] for j in range(U)]
        for h in range(2):
            v_cat = jnp.concatenate([vts[j][h * hd:(h + 1) * hd, :] for j in range(U)], axis=1)
            acc_s[h] += _dot(v_cat, a_s[slot][h])

    def finish_qblock(qi):
        q0 = qi * tq
        o = jnp.concatenate([acc_s[0], acc_s[1]], axis=0).T
        zg = z_ref[pl.ds(q0, tq), :].astype(F32)
        o_ref[pl.ds(q0, tq), :] = (o * _silu(zg)).astype(BF16)
        acc_s[...] = jnp.zeros_like(acc_s)

    def weights(slot, carries, diag):
        out = []
        for h in range(2):
            carry = carries[h]
            for j in reversed(range(U)):
                r0 = (h * U + j) * kb_n
                if not diag:
                    pieces = [chunk(zt_s[slot][r0:r0 + kb_n, l:l + kb_n], None, carry[:, l:l + kb_n])
                              for l in range(0, tq, kb_n)]
                    a = jnp.concatenate([p[0] for p in pieces], axis=1)
                    carry = jnp.concatenate([p[1] for p in pieces], axis=1)
                else:
                    l0 = j * kb_n
                    parts, cparts = [], []
                    if l0 > 0:
                        parts.append(jnp.zeros((kb_n, l0), F32))
                        cparts.append(carry[:, 0:l0])
                    a_m, c_m = chunk(zt_s[slot][r0:r0 + kb_n, l0:l0 + kb_n], base_d, carry[:, l0:l0 + kb_n])
                    parts.append(a_m)
                    cparts.append(c_m)
                    if l0 + kb_n < tq:
                        a_u, c_u = chunk(zt_s[slot][r0:r0 + kb_n, l0 + kb_n:tq], None, carry[:, l0 + kb_n:tq])
                        parts.append(a_u)
                        cparts.append(c_u)
                    a = jnp.concatenate(parts, axis=1)
                    carry = jnp.concatenate(cparts, axis=1)
                a_s[slot][h, j * kb_n:(j + 1) * kb_n, :] = a.astype(BF16)
            out.append(carry)
        return tuple(out)

    items = []
    for qi in range(n_q):
        items.append((qi, qi * U, True))
        items.extend((qi, j * U, False) for j in reversed(range(qi)))

    acc_s[...] = jnp.zeros_like(acc_s)
    issue_scores(items[0][1], items[0][0], 0)
    carries = None
    for n, (qi, kb0, diag) in enumerate(items):
        if n + 1 < len(items):
            issue_scores(items[n + 1][1], items[n + 1][0], (n + 1) % 2)
        if n > 0:
            add_values(items[n - 1][1], (n - 1) % 2)
        if diag:
            if qi > 0:
                finish_qblock(qi - 1)
            one = jnp.ones((SUBLANES, tq), F32)
            carries = (one, one)
        carries = weights(n % 2, carries, diag)
    add_values(items[-1][1], (len(items) - 1) % 2)
    finish_qblock(n_q - 1)


def _sb_attention(proj, batch, seq):
    pairs = SB_WIDTH // LANES
    n_kb = seq // LANES
    blk = lambda off: pl.BlockSpec((seq, LANES), lambda b, p, off=off: (b, off + p))
    return pl.pallas_call(
        functools.partial(_sb_kernel, seq=seq),
        grid=(batch, pairs),
        in_specs=[blk(0), blk(pairs), blk(2 * pairs), blk(3 * pairs)],
        out_specs=pl.BlockSpec((seq, LANES), lambda b, p: (b, p)),
        out_shape=jax.ShapeDtypeStruct((batch * seq, SB_WIDTH), BF16),
        scratch_shapes=[
            pltpu.VMEM((n_kb, LANES, LANES), BF16),
            pltpu.VMEM((n_kb, LANES, LANES), BF16),
            pltpu.VMEM((n_kb, LANES, LANES), BF16),
            pltpu.VMEM((2, SB_HEAD_DIM, SB_TQ), F32),
            pltpu.VMEM((2 * SB_TQ, SB_TQ), F32),
            pltpu.VMEM((2 * SB_TQ, SB_TQ), F32),
            pltpu.VMEM((2, SB_TQ, SB_TQ), BF16),
            pltpu.VMEM((2, SB_TQ, SB_TQ), BF16),
            pltpu.VMEM((seq, LANES), BF16),
        ],
        compiler_params=pltpu.CompilerParams(
            dimension_semantics=("arbitrary", "arbitrary"), vmem_limit_bytes=VMEM_LIMIT),
        name="sb_attention",
    )(proj, proj, proj, proj)


def _mix_kernel(z_ref, xbc_ref, xprev_ref, sm_ref, cw_ref, cb_ref, dtb_ref, alog_ref, dexp_ref, snw_ref,
                e16_ref, tri_ref, shift_ref, q_ref, k_ref, v_ref, gz_ref, gw_ref, gb_ref, gnw_ref,
                sel_ref, msk_ref, ob_ref, oc_ref, h_s, st_s, att_s):
    first = pl.program_id(1) == 0

    @pl.when(first)
    def _():
        h_s[...] = jnp.zeros_like(h_s)
        st_s[...] = jnp.zeros_like(st_s)

    gla = _gla_phases(q_ref, k_ref, v_ref, gz_ref, sm_ref, gw_ref, gb_ref, gnw_ref, tri_ref, sel_ref,
                      msk_ref, oc_ref, st_s, att_s)
    next(gla)
    staged = [_ssd_stage1(sub, first, xbc_ref, xprev_ref, sm_ref, cw_ref, cb_ref, dtb_ref, alog_ref,
                          e16_ref, tri_ref, shift_ref) for sub in range(MIX_SUBCHUNKS)]
    next(gla)
    for sub in range(MIX_SUBCHUNKS):
        _ssd_stage2(sub, staged[sub], z_ref, dexp_ref, snw_ref, ob_ref, h_s)
    for _ in gla:
        pass


def _ssd_stage1(sub, first, xbc_ref, xprev_ref, sm_ref, cw_ref, cb_ref, dtb_ref, alog_ref,
                e16_ref, tri_ref, shift_ref):
    L = CHUNK
    rows = slice(sub * L, (sub + 1) * L)

    cur = xbc_ref[rows, :]
    if sub == 0:
        prev = xprev_ref[...]
        prev = jnp.where(first, jnp.zeros_like(prev), prev)
    else:
        prev = xbc_ref[(sub - 1) * L:sub * L, :]
    both = jnp.concatenate([prev, cur], axis=0)
    shifted = _dot(shift_ref[...], both)
    acc = cb_ref[...] + cw_ref[SSD_CONV - 1:SSD_CONV, :] * cur.astype(F32)
    for j in range(SSD_CONV - 1):
        acc = acc + cw_ref[j:j + 1, :] * shifted[j * L:(j + 1) * L, :]
    xbc = _silu(acc)
    xs = xbc[:, 0:SSD_WIDTH]

    lane = lax.broadcasted_iota(jnp.int32, (L, LANES), 1)
    head_lanes = lane < SSD_HEADS
    dt = jnp.where(head_lanes, _softplus(sm_ref[rows, :] + dtb_ref[...]), 0.0)
    a = dt * (-jnp.exp(alog_ref[...]))
    tri = tri_ref[...]
    a1, a2, a3 = _split3(a)
    cs = _dot(tri, a1) + _dot(tri, a2) + _dot(tri, a3)
    cs_t = cs.T
    ecs = jnp.exp(cs)
    dend = jnp.exp(cs[L - 1:L, :] - cs)
    stack = jnp.concatenate([dt, dt * dend, ecs], axis=0)
    s_hi, s_lo = _split2(stack)
    e16 = e16_ref[...]
    ex = _dot(s_hi, e16) + _dot(s_lo, e16)
    x_dt = (xs * ex[0:L, :]).astype(BF16)
    x_st = (xs * ex[L:2 * L, :]).astype(BF16)
    ecs_x = ex[2 * L:3 * L, :]
    return xbc, xs, cs, cs_t, x_dt, x_st, ecs_x


def _ssd_stage2(sub, staged, z_ref, dexp_ref, nw_ref, o_ref, h_s):
    xbc, xs, cs, cs_t, x_dt, x_st, ecs_x = staged
    L = CHUNK
    gw = SSD_WIDTH // SSD_GROUPS
    hpg = SSD_HEADS // SSD_GROUPS
    n = SSD_STATE
    rows = slice(sub * L, (sub + 1) * L)

    rowi = lax.broadcasted_iota(jnp.int32, (L, L), 0)
    coli = lax.broadcasted_iota(jnp.int32, (L, L), 1)
    causal = rowi >= coli
    lo_lanes = coli < SSD_HEAD_DIM

    y_parts = []
    for g in range(SSD_GROUPS):
        bm = xbc[:, SSD_WIDTH + g * n:SSD_WIDTH + (g + 1) * n].astype(BF16)
        cm = xbc[:, SSD_WIDTH + SSD_GROUPS * n + g * n:SSD_WIDTH + SSD_GROUPS * n + (g + 1) * n].astype(BF16)
        scores = _dot_nt(cm, bm)
        for pr in range(hpg // 2):
            col = g * gw + pr * LANES
            xp = x_dt[:, col:col + LANES]
            yd = None
            for half in range(2):
                h = g * hpg + 2 * pr + half
                sg = cs[:, h:h + 1] - cs_t[h:h + 1, :]
                dec = jnp.where(causal, jnp.exp(jnp.where(causal, sg, 0.0)), 0.0)
                w = (scores * dec).astype(BF16)
                xm = jnp.where(lo_lanes, xp, jnp.zeros_like(xp)) if half == 0 else jnp.where(
                    lo_lanes, jnp.zeros_like(xp), xp)
                t = _dot(w, xm)
                yd = t if yd is None else yd + t
            y_parts.append(yd)
        h_prev = h_s[g]
        y_off = _dot(cm, h_prev.astype(BF16)) * ecs_x[:, g * gw:(g + 1) * gw]
        y_parts.append(y_off)
        st_new = _dot_tn(bm, x_st[:, g * gw:(g + 1) * gw])
        h_s[g] = h_prev * ecs_x[L - 1:L, g * gw:(g + 1) * gw] + st_new

    per_g = 1 + hpg // 2
    outs = []
    zg = z_ref[rows, :].astype(F32)
    for g in range(SSD_GROUPS):
        yd = jnp.concatenate(y_parts[g * per_g:g * per_g + hpg // 2], axis=1)
        y = yd + y_parts[g * per_g + hpg // 2] + dexp_ref[:, g * gw:(g + 1) * gw] * xs[:, g * gw:(g + 1) * gw]
        zz = zg[:, g * gw:(g + 1) * gw]
        y = y * _silu(zz)
        ms = jnp.mean(y * y, axis=-1, keepdims=True)
        outs.append(y * lax.rsqrt(ms + NORM_EPS) * nw_ref[:, g * gw:(g + 1) * gw])
    o_ref[rows, :] = jnp.concatenate(outs, axis=1).astype(BF16)


def _gla_phases(q_ref, k_ref, v_ref, z_ref, sm_ref, gw_ref, gb_ref, nw_ref, tri_ref, sel_ref,
                msk_ref, o_ref, st_s, att_s):
    L = CHUNK
    kk = GLA_HEAD_K
    vv = GLA_HEAD_V

    lane = lax.broadcasted_iota(jnp.int32, (L, GLA_KEY), 1)
    rowi = lax.broadcasted_iota(jnp.int32, (L, GLA_KEY), 0)
    head_of_lane = lane // kk

    def head_mask(xb, h):
        return jnp.where(head_of_lane == h, xb, jnp.zeros_like(xb))

    g_hi, g_lo = _split2(gw_ref[...])
    tri = tri_ref[...]
    subs = []
    for sub in range(MIX_SUBCHUNKS):
        rows = slice(sub * L, (sub + 1) * L)
        s_hi, s_lo = _split2(sm_ref[rows, :])
        u = _dot(s_hi, g_hi) + _dot(s_lo, g_hi) + _dot(s_hi, g_lo) + gb_ref[...]
        lg = _log_sigmoid(u) * (1.0 / GLA_GATE_TAU)
        l1, l2, l3 = _split3(lg)
        gc = _dot(tri, l1) + _dot(tri, l2) + _dot(tri, l3)
        q = q_ref[rows, :].astype(F32) * (kk ** -0.5)
        k = k_ref[rows, :].astype(F32)
        g_last = gc[L - 1:L, :]
        q_in = (q * jnp.exp(gc)).astype(BF16)
        k_dec = (k * jnp.exp(g_last - gc)).astype(BF16)
        k_growth = jnp.max(jnp.abs(k), axis=0, keepdims=True) * jnp.exp(-g_last)
        subs.append(dict(rows=rows, gc=gc, q=q, k=k, g_last=g_last, q_in=q_in, k_dec=k_dec, growth=k_growth))

    growth = subs[0]["growth"]
    for s in subs[1:]:
        growth = jnp.maximum(growth, s["growth"])
    direct = jnp.max(growth) < GLA_DIRECT_MAX
    yield

    @pl.when(direct)
    def _():
        m = msk_ref[GLA_LEVELS + 1] > 0.0
        for i, s in enumerate(subs):
            k_up = (s["k"] * jnp.exp(-s["gc"])).astype(BF16)
            for h in range(GLA_HEADS):
                att_s[i, h] = jnp.where(m, _dot_nt(head_mask(s["q_in"], h), k_up), 0.0)

    @pl.when(jnp.logical_not(direct))
    def _():
        sel = sel_ref[...]
        for i, s in enumerate(subs):
            gc, q, k = s["gc"], s["q"], s["k"]
            c1, c2, c3 = _split3(gc)
            ref_g = _dot(sel, c1) + _dot(sel, c2) + _dot(sel, c3)
            att = [None] * GLA_HEADS
            for lv in range(GLA_LEVELS):
                expo = -jnp.abs(gc - ref_g[lv * L:(lv + 1) * L, :])
                upper = ((rowi >> lv) & 1) == 1
                xb = (jnp.where(upper, q, k) * jnp.exp(expo)).astype(BF16)
                m = msk_ref[lv]
                for h in range(GLA_HEADS):
                    p = _dot_nt(head_mask(xb, h), xb) * m
                    att[h] = p if att[h] is None else att[h] + p
            qb = q.astype(BF16)
            kb = k.astype(BF16)
            m = msk_ref[GLA_LEVELS]
            for h in range(GLA_HEADS):
                att_s[i, h] = att[h] + _dot_nt(head_mask(qb, h), kb) * m

    yield
    lane_s = lax.broadcasted_iota(jnp.int32, (vv, GLA_KEY), 1) // kk
    st = st_s[...]
    for i, s in enumerate(subs):
        rows = s["rows"]
        st_b = st.astype(BF16)
        st_new = st * jnp.exp(s["g_last"])
        outs = []
        for h in range(GLA_HEADS):
            vh = v_ref[rows, h * vv:(h + 1) * vv]
            o = _dot_nt(head_mask(s["q_in"], h), st_b) + _dot(att_s[i, h].astype(BF16), vh)
            upd = _dot_tn(vh, s["k_dec"])
            st_new = st_new + jnp.where(lane_s == h, upd, 0.0)
            ms = jnp.mean(o * o, axis=-1, keepdims=True)
            o = o * lax.rsqrt(ms + NORM_EPS) * nw_ref[...]
            zz = z_ref[rows, h * vv:(h + 1) * vv].astype(F32)
            outs.append(o * _silu(zz))
        o_ref[rows, :] = jnp.concatenate(outs, axis=1).astype(BF16)
        st = st_new
    st_s[...] = st


def _ssd_gla(proj, small, cw, cb, dtb, alog, dexp, snw, e16, tri, shift, gw, gb, gnw, sel, msk, batch, seq):
    step = MIX_SUBCHUNKS * CHUNK
    nc = seq // step
    row = lambda b, c: b * nc + c
    prev_row = lambda b, c: b * (seq // CHUNK) + jnp.maximum(c * MIX_SUBCHUNKS - 1, 0)
    const = lambda shape: pl.BlockSpec(shape, lambda b, c: (0,) * len(shape))
    rows = lambda width, off: pl.BlockSpec((step, width), lambda b, c: (row(b, c), off // width))
    return pl.pallas_call(
        _mix_kernel,
        grid=(batch, nc),
        in_specs=[
            rows(SSD_WIDTH, OFF_SSD_Z), rows(SSD_CONV_DIM, OFF_SSD_XBC),
            pl.BlockSpec((CHUNK, SSD_CONV_DIM), lambda b, c: (prev_row(b, c), OFF_SSD_XBC // SSD_CONV_DIM)),
            rows(LANES, 0),
            const((SSD_CONV, SSD_CONV_DIM)), const((1, SSD_CONV_DIM)), const((1, LANES)), const((1, LANES)),
            const((1, SSD_WIDTH)), const((1, SSD_WIDTH)), const((LANES, SSD_WIDTH)), const((CHUNK, CHUNK)),
            const(((SSD_CONV - 1) * CHUNK, 2 * CHUNK)),
            rows(GLA_KEY, OFF_GLA_Q), rows(GLA_KEY, OFF_GLA_K), rows(GLA_WIDTH, OFF_GLA_V), rows(GLA_WIDTH, OFF_GLA_Z),
            const((LANES, GLA_KEY)), const((1, GLA_KEY)), const((1, GLA_HEAD_V)),
            const((GLA_LEVELS * CHUNK, CHUNK)), const((GLA_LEVELS + 2, CHUNK, CHUNK)),
        ],
        out_specs=[pl.BlockSpec((step, SSD_WIDTH), lambda b, c: (row(b, c), 0)),
                   pl.BlockSpec((step, GLA_WIDTH), lambda b, c: (row(b, c), 0))],
        out_shape=[jax.ShapeDtypeStruct((batch * seq, SSD_WIDTH), BF16),
                   jax.ShapeDtypeStruct((batch * seq, GLA_WIDTH), BF16)],
        scratch_shapes=[pltpu.VMEM((SSD_GROUPS, SSD_STATE, SSD_WIDTH // SSD_GROUPS), F32),
                        pltpu.VMEM((GLA_HEAD_V, GLA_KEY), F32),
                        pltpu.VMEM((MIX_SUBCHUNKS, GLA_HEADS, CHUNK, CHUNK), F32)],
        compiler_params=pltpu.CompilerParams(
            dimension_semantics=("arbitrary", "arbitrary"), vmem_limit_bytes=VMEM_LIMIT),
        name="ssd_gla",
    )(proj, proj, proj, small, cw, cb, dtb, alog, dexp, snw, e16, tri, shift,
      proj, proj, proj, proj, gw, gb, gnw, sel, msk)


def _outproj_kernel(x_ref, ya_ref, yb_ref, yc_ref, w_ref, fw_ref, o_ref, *, final):
    y = _dot(ya_ref[...], w_ref[0:SB_WIDTH, :])
    y = y + _dot(yb_ref[...], w_ref[SB_WIDTH:SB_WIDTH + SSD_WIDTH, :])
    y = y + _dot(yc_ref[...], w_ref[SB_WIDTH + SSD_WIDTH:D_INNER, :])
    xn = x_ref[...] + y
    if final:
        ms = jnp.mean(xn * xn, axis=-1, keepdims=True)
        xn = xn * lax.rsqrt(ms + NORM_EPS) * fw_ref[...]
    o_ref[...] = xn


def _outproj(x2, ya, yb, yc, w, fw, *, final, tm=512):
    t = x2.shape[0]
    rows = lambda width: pl.BlockSpec((tm, width), lambda i: (i, 0))
    return pl.pallas_call(
        functools.partial(_outproj_kernel, final=final),
        grid=(t // tm,),
        in_specs=[rows(D_MODEL), rows(SB_WIDTH), rows(SSD_WIDTH), rows(GLA_WIDTH),
                  pl.BlockSpec((D_INNER, D_MODEL), lambda i: (0, 0)),
                  pl.BlockSpec((1, D_MODEL), lambda i: (0, 0))],
        out_specs=rows(D_MODEL),
        out_shape=jax.ShapeDtypeStruct((t, D_MODEL), F32),
        compiler_params=pltpu.CompilerParams(
            dimension_semantics=("arbitrary",), vmem_limit_bytes=VMEM_LIMIT),
        name="outproj_final" if final else "outproj",
    )(x2, ya, yb, yc, w, fw)


def _constants():
    L = CHUNK
    t = np.arange(L)
    tri = (t[:, None] >= t[None, :]).astype(np.float32)
    sel = np.zeros((GLA_LEVELS * L, L), np.float32)
    msk = np.zeros((GLA_LEVELS + 2, L, L), np.float32)
    msk[GLA_LEVELS + 1] = tri
    shift = np.zeros(((SSD_CONV - 1) * L, 2 * L), np.float32)
    for j in range(SSD_CONV - 1):
        shift[j * L + t, L + t - (SSD_CONV - 1 - j)] = 1.0
    for lv in range(GLA_LEVELS):
        mid = ((t >> (lv + 1)) << (lv + 1)) + (1 << lv)
        sel[lv * L + t, mid - 1] = 1.0
        same = (t[:, None] >> (lv + 1)) == (t[None, :] >> (lv + 1))
        up = ((t[:, None] >> lv) & 1) == 1
        low = ((t[None, :] >> lv) & 1) == 0
        msk[lv] = (same & up & low).astype(np.float32)
    msk[GLA_LEVELS] = np.eye(L, dtype=np.float32)
    e16 = np.zeros((LANES, SSD_WIDTH), np.float32)
    for h in range(SSD_HEADS):
        e16[h, h * SSD_HEAD_DIM:(h + 1) * SSD_HEAD_DIM] = 1.0
    return (jnp.asarray(tri, BF16), jnp.asarray(sel, BF16), jnp.asarray(msk, F32), jnp.asarray(e16, BF16),
            jnp.asarray(shift, BF16))


def _pad_lanes(v, offset=0):
    out = jnp.zeros((1, LANES), F32)
    return out.at[0, offset:offset + v.shape[0]].set(v.astype(F32))


def kernel(x, norm_w, w_in, ssd_conv_w, ssd_conv_b, ssd_dt_bias, ssd_a_log, ssd_d, ssd_norm_w,
           gla_gate_w, gla_gate_b, gla_norm_w, w_out, final_norm_w):
    batch, seq, _ = x.shape
    assert seq % 256 == 0 and (batch * seq) % 512 == 0
    tri, sel, msk, e16, shift = _constants()
    x2 = x.reshape(batch * seq, D_MODEL).astype(F32)
    dt_col = OFF_GLA_Q
    glr_col = dt_col + SSD_HEADS + 2 * GLA_KEY + 2 * GLA_WIDTH
    for layer in range(DEPTH):
        w = w_in[layer]
        wa = w[:, :dt_col].astype(BF16)
        wb = w[:, dt_col + SSD_HEADS:glr_col].astype(BF16)
        ws = jnp.concatenate(
            [w[:, dt_col:dt_col + SSD_HEADS], w[:, glr_col:glr_col + GLA_GATE_RANK],
             jnp.zeros((D_MODEL, LANES - SSD_HEADS - GLA_GATE_RANK), w.dtype)], axis=1).astype(BF16)
        proj, small = _inproj(x2, norm_w[layer].reshape(1, D_MODEL), wa, wb, ws)

        ya = _sb_attention(proj, batch, seq)
        gw = jnp.zeros((LANES, GLA_KEY), F32).at[SSD_HEADS:SSD_HEADS + GLA_GATE_RANK, :].set(gla_gate_w[layer])
        yb, yc = _ssd_gla(proj, small, ssd_conv_w[layer], ssd_conv_b[layer].reshape(1, SSD_CONV_DIM),
                          _pad_lanes(ssd_dt_bias[layer]), _pad_lanes(ssd_a_log[layer]),
                          jnp.repeat(ssd_d[layer], SSD_HEAD_DIM).reshape(1, SSD_WIDTH),
                          ssd_norm_w[layer].reshape(1, SSD_WIDTH), e16, tri, shift,
                          gw, gla_gate_b[layer].reshape(1, GLA_KEY), gla_norm_w[layer].reshape(1, GLA_HEAD_V),
                          sel, msk, batch, seq)
        x2 = _outproj(x2, ya, yb, yc, w_out[layer].astype(BF16), final_norm_w.reshape(1, D_MODEL),
                      final=(layer == DEPTH - 1))
    return x2.reshape(batch, seq, D_MODEL).astype(x.dtype)
```

```python
import functools

import jax
import jax.numpy as jnp
import numpy as np
from jax import lax
from jax.experimental import pallas as pl
from jax.experimental.pallas import tpu as pltpu

F32 = jnp.float32
BF16 = jnp.bfloat16

D_MODEL = 1024
DEPTH = 2
D_INNER = 2 * D_MODEL
SB_WIDTH = D_INNER // 4
SB_HEAD_DIM = 64
SSD_WIDTH = D_INNER // 2
SSD_HEAD_DIM = 64
SSD_HEADS = SSD_WIDTH // SSD_HEAD_DIM
SSD_GROUPS = 2
SSD_STATE = 128
SSD_CONV = 4
SSD_CONV_DIM = SSD_WIDTH + 2 * SSD_GROUPS * SSD_STATE
GLA_WIDTH = D_INNER // 4
GLA_HEADS = 4
GLA_KEY = GLA_WIDTH // 2
GLA_HEAD_K = GLA_KEY // GLA_HEADS
GLA_HEAD_V = GLA_WIDTH // GLA_HEADS
GLA_GATE_RANK = 16
GLA_GATE_TAU = 16.0
NORM_EPS = 1e-6

LANES = 128
SUBLANES = 8
CHUNK = 128
MIX_SUBCHUNKS = 4
N_MAIN = 4 * SB_WIDTH + SSD_WIDTH + SSD_CONV_DIM + 2 * GLA_KEY + 2 * GLA_WIDTH
OFF_SSD_Z = 4 * SB_WIDTH
OFF_SSD_XBC = OFF_SSD_Z + SSD_WIDTH
OFF_GLA_Q = OFF_SSD_XBC + SSD_CONV_DIM
OFF_GLA_K = OFF_GLA_Q + GLA_KEY
OFF_GLA_V = OFF_GLA_K + GLA_KEY
OFF_GLA_Z = OFF_GLA_V + GLA_WIDTH
GLA_LEVELS = 7
GLA_DIRECT_MAX = 1e30
SB_TQ = 512
LOG2E = 1.4426950408889634
VMEM_LIMIT = 56 * 1024 * 1024


def _dot(a, b):
    return jnp.dot(a, b, preferred_element_type=F32)


def _dot_nt(a, b):
    return lax.dot_general(a, b, (((1,), (1,)), ((), ())), preferred_element_type=F32)


def _dot_tn(a, b):
    return lax.dot_general(a, b, (((0,), (0,)), ((), ())), preferred_element_type=F32)


def _split2(x):
    hi = x.astype(BF16)
    lo = (x - hi.astype(F32)).astype(BF16)
    return hi, lo


def _split3(x):
    hi = x.astype(BF16)
    r = x - hi.astype(F32)
    mid = r.astype(BF16)
    lo = (r - mid.astype(F32)).astype(BF16)
    return hi, mid, lo


def _silu(x):
    hx = 0.5 * x
    return hx * jnp.tanh(hx) + hx


def _log_sigmoid(x):
    return jnp.minimum(x, 0.0) - jnp.log(1.0 + jnp.exp(-jnp.abs(x)))


def _softplus(x):
    return jnp.maximum(x, 0.0) + jnp.log(1.0 + jnp.exp(-jnp.abs(x)))


def _inproj_kernel(x_ref, nw_ref, wa_ref, wb_ref, ws_ref, o_ref, os_ref, *, n_chunk):
    x = x_ref[...]
    ms = jnp.mean(x * x, axis=-1, keepdims=True)
    h = (x * lax.rsqrt(ms + NORM_EPS) * nw_ref[...]).astype(BF16)
    for c0 in range(0, OFF_GLA_Q, n_chunk):
        o_ref[:, c0:c0 + n_chunk] = _dot(h, wa_ref[:, c0:c0 + n_chunk]).astype(BF16)
    for c0 in range(0, N_MAIN - OFF_GLA_Q, n_chunk):
        o_ref[:, OFF_GLA_Q + c0:OFF_GLA_Q + c0 + n_chunk] = _dot(h, wb_ref[:, c0:c0 + n_chunk]).astype(BF16)
    os_ref[...] = _dot(h, ws_ref[...])


def _inproj(x2, nw, wa, wb, ws, *, tm=512, n_chunk=1536):
    t = x2.shape[0]
    wspec = lambda cols: pl.BlockSpec((D_MODEL, cols), lambda i: (0, 0))
    return pl.pallas_call(
        functools.partial(_inproj_kernel, n_chunk=n_chunk),
        grid=(t // tm,),
        in_specs=[
            pl.BlockSpec((tm, D_MODEL), lambda i: (i, 0)),
            pl.BlockSpec((1, D_MODEL), lambda i: (0, 0)),
            wspec(OFF_GLA_Q), wspec(N_MAIN - OFF_GLA_Q), wspec(LANES),
        ],
        out_specs=[
            pl.BlockSpec((tm, N_MAIN), lambda i: (i, 0)),
            pl.BlockSpec((tm, LANES), lambda i: (i, 0)),
        ],
        out_shape=[
            jax.ShapeDtypeStruct((t, N_MAIN), BF16),
            jax.ShapeDtypeStruct((t, LANES), F32),
        ],
        compiler_params=pltpu.CompilerParams(
            dimension_semantics=("arbitrary",), vmem_limit_bytes=VMEM_LIMIT),
        name="inproj",
    )(x2, nw, wa, wb, ws)


def _sb_kernel(*refs, seq, far):
    if far:
        q_ref, k_ref, v_ref, z_ref, yin_ref, car_ref, o_ref = refs[:7]
    else:
        q_ref, k_ref, v_ref, z_ref, o_ref, car_ref = refs[:6]
    k0_s, k1_s, vt_s, acc_s, zt0_s, zt1_s, a0_s, a1_s, qs_s = refs[-9:]
    zt_s = (zt0_s, zt1_s)
    a_s = (a0_s, a1_s)
    kb_n = LANES
    n_kb = seq // kb_n
    groups = kb_n // SUBLANES
    hd = SB_HEAD_DIM
    tq = SB_TQ
    U = tq // kb_n

    lane = lax.broadcasted_iota(jnp.int32, (kb_n, kb_n), 1)
    row = lax.broadcasted_iota(jnp.int32, (kb_n, kb_n), 0)
    src = (row % SUBLANES) * groups + row // SUBLANES
    perm = jnp.where(lane == src, 1.0, 0.0).astype(BF16)
    src_t = (lane % SUBLANES) * groups + lane // SUBLANES
    perm_t = jnp.where(row == src_t, 1.0, 0.0).astype(BF16)
    lo_lanes = lane < hd

    def prep(n, carry):
        for j in range(U):
            kb = n * U + j
            r0 = pl.multiple_of(kb * kb_n, kb_n)
            kp = _dot(perm, k_ref[pl.ds(r0, kb_n), :]).astype(BF16)
            k0_s[kb] = jnp.where(lo_lanes, kp, jnp.zeros_like(kp))
            k1_s[kb] = jnp.where(lo_lanes, jnp.zeros_like(kp), kp)
            vt = v_ref[pl.ds(r0, kb_n), :].astype(F32).T.astype(BF16)
            vt_s[kb] = _dot(vt, perm_t).astype(BF16)
            qs_s[pl.ds(r0, kb_n), :] = (q_ref[pl.ds(r0, kb_n), :].astype(F32) * (0.5 * hd ** -0.5)).astype(BF16)
        return carry

    lax.fori_loop(0, n_kb // U, prep, 0)

    def block_scan(zt, base):
        nl = zt.shape[1]
        seg = lax.broadcasted_iota(jnp.int32, (SUBLANES, nl), 0)
        run = jnp.ones((SUBLANES, nl), F32)
        s_rows = [None] * groups
        for i in reversed(range(groups)):
            zh = zt[i * SUBLANES:(i + 1) * SUBLANES, :]
            beta = 0.5 * jnp.tanh(zh) + 0.5
            if base is not None:
                beta = jnp.where(i < base, beta, 0.0)
            s_rows[i] = beta * run
            run = run - s_rows[i]
        inc = run
        for sh in (1, 2, 4):
            inc = inc * jnp.where(seg < SUBLANES - sh, pltpu.roll(inc, SUBLANES - sh, 0), 1.0)
        excl = jnp.where(seg < SUBLANES - 1, pltpu.roll(inc, SUBLANES - 1, 0), 1.0)
        total = jnp.broadcast_to(inc[0:1, :], (SUBLANES, nl))
        return s_rows, excl, total

    def chunk(ztc, base, cc):
        s_rows, excl, total = block_scan(ztc, base)
        off = excl * cc
        a = jnp.concatenate([s_rows[i] * off for i in range(groups)], axis=0)
        return a, cc * total

    seg_d = lax.broadcasted_iota(jnp.int32, (SUBLANES, kb_n), 0)
    lane_d = lax.broadcasted_iota(jnp.int32, (SUBLANES, kb_n), 1)
    base_d = lane_d - seg_d * groups

    n_q = seq // tq

    def issue_scores(kb0, qi, slot):
        qb = qs_s[qi * tq:(qi + 1) * tq, :]
        k_cat = jnp.concatenate([k0_s[kb0 + j] for j in range(U)] + [k1_s[kb0 + j] for j in range(U)], axis=0)
        zt_s[slot][...] = _dot_nt(k_cat, qb)

    def add_values(kb0, slot):
        vts = [vt_s[kb0 + j] for j in range(U)]
        for h in range(2):
            v_cat = jnp.concatenate([vts[j][h * hd:(h + 1) * hd, :] for j in range(U)], axis=1)
            acc_s[h] += _dot(v_cat, a_s[slot][h])

    def finish_qblock(qi):
        q0 = qi * tq
        o = jnp.concatenate([acc_s[0], acc_s[1]], axis=0).T
        zg = z_ref[pl.ds(q0, tq), :].astype(F32)
        y = o * _silu(zg)
        if far:
            y = y + yin_ref[pl.ds(q0, tq), :].astype(F32)
        o_ref[pl.ds(q0, tq), :] = y.astype(BF16)
        acc_s[...] = jnp.zeros_like(acc_s)

    def weights(slot, carries, diag):
        out = []
        for h in range(2):
            carry = carries[h]
            for j in reversed(range(U)):
                r0 = (h * U + j) * kb_n
                if not diag:
                    pieces = [chunk(zt_s[slot][r0:r0 + kb_n, l:l + kb_n], None, carry[:, l:l + kb_n])
                              for l in range(0, tq, kb_n)]
                    a = jnp.concatenate([p[0] for p in pieces], axis=1)
                    carry = jnp.concatenate([p[1] for p in pieces], axis=1)
                else:
                    l0 = j * kb_n
                    parts, cparts = [], []
                    if l0 > 0:
                        parts.append(jnp.zeros((kb_n, l0), F32))
                        cparts.append(carry[:, 0:l0])
                    a_m, c_m = chunk(zt_s[slot][r0:r0 + kb_n, l0:l0 + kb_n], base_d, carry[:, l0:l0 + kb_n])
                    parts.append(a_m)
                    cparts.append(c_m)
                    if l0 + kb_n < tq:
                        a_u, c_u = chunk(zt_s[slot][r0:r0 + kb_n, l0 + kb_n:tq], None, carry[:, l0 + kb_n:tq])
                        parts.append(a_u)
                        cparts.append(c_u)
                    a = jnp.concatenate(parts, axis=1)
                    carry = jnp.concatenate(cparts, axis=1)
                a_s[slot][h, j * kb_n:(j + 1) * kb_n, :] = a.astype(BF16)
            out.append(carry)
        return tuple(out)

    items = []
    for qi in range(n_q):
        if far:
            items.extend((qi, j * U, False) for j in reversed(range(qi - 1)))
        else:
            items.append((qi, qi * U, True))
            if qi > 0:
                items.append((qi, (qi - 1) * U, False))
    first_item = {}
    for n, it in enumerate(items):
        first_item.setdefault(it[0], n)
    if far:
        for qi in range(n_q):
            if qi not in first_item:
                q0 = qi * tq
                o_ref[pl.ds(q0, tq), :] = yin_ref[pl.ds(q0, tq), :]

    acc_s[...] = jnp.zeros_like(acc_s)
    issue_scores(items[0][1], items[0][0], 0)
    carries = None
    for n, (qi, kb0, diag) in enumerate(items):
        if n + 1 < len(items):
            issue_scores(items[n + 1][1], items[n + 1][0], (n + 1) % 2)
        if n > 0:
            add_values(items[n - 1][1], (n - 1) % 2)
        if first_item[qi] == n:
            if n > 0:
                finish_qblock(items[n - 1][0])
            if far:
                carries = (car_ref[qi, 0], car_ref[qi, 1])
            else:
                one = jnp.ones((SUBLANES, tq), F32)
                carries = (one, one)
        carries = weights(n % 2, carries, diag)
        if not far and (n + 1 == len(items) or items[n + 1][0] != qi):
            car_ref[qi, 0] = carries[0]
            car_ref[qi, 1] = carries[1]
    add_values(items[-1][1], (len(items) - 1) % 2)
    finish_qblock(items[-1][0])


def _sb_call(proj, batch, seq, far, extra):
    pairs = SB_WIDTH // LANES
    n_kb = seq // LANES
    n_q = seq // SB_TQ
    blk = lambda off: pl.BlockSpec((seq, LANES), lambda b, p, off=off: (b, off + p))
    y_spec = pl.BlockSpec((seq, LANES), lambda b, p: (b, p))
    car_spec = pl.BlockSpec((None, None, n_q, 2, SUBLANES, SB_TQ), lambda b, p: (b, p, 0, 0, 0, 0))
    y_shape = jax.ShapeDtypeStruct((batch * seq, SB_WIDTH), BF16)
    car_shape = jax.ShapeDtypeStruct((batch, pairs, n_q, 2, SUBLANES, SB_TQ), F32)
    return pl.pallas_call(
        functools.partial(_sb_kernel, seq=seq, far=far),
        grid=(batch, pairs),
        in_specs=[blk(0), blk(pairs), blk(2 * pairs), blk(3 * pairs)] + ([y_spec, car_spec] if far else []),
        out_specs=y_spec if far else [y_spec, car_spec],
        out_shape=y_shape if far else [y_shape, car_shape],
        scratch_shapes=[
            pltpu.VMEM((n_kb, LANES, LANES), BF16),
            pltpu.VMEM((n_kb, LANES, LANES), BF16),
            pltpu.VMEM((n_kb, LANES, LANES), BF16),
            pltpu.VMEM((2, SB_HEAD_DIM, SB_TQ), F32),
            pltpu.VMEM((2 * SB_TQ, SB_TQ), F32),
            pltpu.VMEM((2 * SB_TQ, SB_TQ), F32),
            pltpu.VMEM((2, SB_TQ, SB_TQ), BF16),
            pltpu.VMEM((2, SB_TQ, SB_TQ), BF16),
            pltpu.VMEM((seq, LANES), BF16),
        ],
        compiler_params=pltpu.CompilerParams(
            dimension_semantics=("arbitrary", "arbitrary"), vmem_limit_bytes=VMEM_LIMIT),
        name="sb_far" if far else "sb_near",
    )(proj, proj, proj, proj, *extra)


def _sb_attention(proj, batch, seq):
    ya, car = _sb_call(proj, batch, seq, False, ())
    if seq // SB_TQ <= 2:
        return ya
    alive = jnp.max(car[:, :, 2:]) > 0.0
    return lax.cond(alive, lambda: _sb_call(proj, batch, seq, True, (ya, car)), lambda: ya)


def _mix_kernel(z_ref, xbc_ref, xprev_ref, sm_ref, cw_ref, cb_ref, dtb_ref, alog_ref, dexp_ref, snw_ref,
                e16_ref, tri_ref, shift_ref, q_ref, k_ref, v_ref, gz_ref, gw_ref, gb_ref, gnw_ref,
                sel_ref, msk_ref, ob_ref, oc_ref, h_s, st_s, att_s):
    first = pl.program_id(1) == 0

    @pl.when(first)
    def _():
        h_s[...] = jnp.zeros_like(h_s)
        st_s[...] = jnp.zeros_like(st_s)

    gla = _gla_phases(q_ref, k_ref, v_ref, gz_ref, sm_ref, gw_ref, gb_ref, gnw_ref, tri_ref, sel_ref,
                      msk_ref, oc_ref, st_s, att_s)
    next(gla)
    staged = [_ssd_stage1(sub, first, xbc_ref, xprev_ref, sm_ref, cw_ref, cb_ref, dtb_ref, alog_ref,
                          e16_ref, tri_ref, shift_ref) for sub in range(MIX_SUBCHUNKS)]
    next(gla)
    for sub in range(MIX_SUBCHUNKS):
        _ssd_stage2(sub, staged[sub], z_ref, dexp_ref, snw_ref, ob_ref, h_s)
    for _ in gla:
        pass


def _ssd_stage1(sub, first, xbc_ref, xprev_ref, sm_ref, cw_ref, cb_ref, dtb_ref, alog_ref,
                e16_ref, tri_ref, shift_ref):
    L = CHUNK
    rows = slice(sub * L, (sub + 1) * L)

    cur = xbc_ref[rows, :]
    if sub == 0:
        prev = xprev_ref[...]
        prev = jnp.where(first, jnp.zeros_like(prev), prev)
    else:
        prev = xbc_ref[(sub - 1) * L:sub * L, :]
    both = jnp.concatenate([prev, cur], axis=0)
    shifted = _dot(shift_ref[...], both)
    acc = cb_ref[...] + cw_ref[SSD_CONV - 1:SSD_CONV, :] * cur.astype(F32)
    for j in range(SSD_CONV - 1):
        acc = acc + cw_ref[j:j + 1, :] * shifted[j * L:(j + 1) * L, :]
    xbc = _silu(acc)
    xs = xbc[:, 0:SSD_WIDTH]

    lane = lax.broadcasted_iota(jnp.int32, (L, LANES), 1)
    head_lanes = lane < SSD_HEADS
    dt = jnp.where(head_lanes, _softplus(sm_ref[rows, :] + dtb_ref[...]), 0.0)
    a = dt * (-jnp.exp(alog_ref[...]))
    tri = tri_ref[...]
    a1, a2, a3 = _split3(a)
    cs = _dot(tri, a1) + _dot(tri, a2) + _dot(tri, a3)
    ecs = jnp.exp(cs)
    dend = jnp.exp(cs[L - 1:L, :] - cs)
    stack = jnp.concatenate([dt, dt * dend, ecs], axis=0)
    s_hi, s_lo = _split2(stack)
    e16 = e16_ref[...]
    ex = _dot(s_hi, e16) + _dot(s_lo, e16)
    x_dt = (xs * ex[0:L, :]).astype(BF16)
    x_st = (xs * ex[L:2 * L, :]).astype(BF16)
    ecs_x = ex[2 * L:3 * L, :]
    cs2 = cs * LOG2E
    return xbc, xs, cs2, cs2.T, x_dt, x_st, ecs_x


def _ssd_stage2(sub, staged, z_ref, dexp_ref, nw_ref, o_ref, h_s):
    xbc, xs, cs, cs_t, x_dt, x_st, ecs_x = staged
    L = CHUNK
    gw = SSD_WIDTH // SSD_GROUPS
    hpg = SSD_HEADS // SSD_GROUPS
    n = SSD_STATE
    rows = slice(sub * L, (sub + 1) * L)

    rowi = lax.broadcasted_iota(jnp.int32, (L, L), 0)
    coli = lax.broadcasted_iota(jnp.int32, (L, L), 1)
    causal = rowi >= coli
    lo_lanes = coli < SSD_HEAD_DIM

    y_parts = []
    for g in range(SSD_GROUPS):
        bm = xbc[:, SSD_WIDTH + g * n:SSD_WIDTH + (g + 1) * n].astype(BF16)
        cm = xbc[:, SSD_WIDTH + SSD_GROUPS * n + g * n:SSD_WIDTH + SSD_GROUPS * n + (g + 1) * n].astype(BF16)
        scores = _dot_nt(cm, bm)
        for pr in range(hpg // 2):
            col = g * gw + pr * LANES
            xp = x_dt[:, col:col + LANES]
            yd = None
            for half in range(2):
                h = g * hpg + 2 * pr + half
                sg = cs[:, h:h + 1] - cs_t[h:h + 1, :]
                dec = jnp.where(causal, jnp.exp2(sg), 0.0)
                w = (scores * dec).astype(BF16)
                xm = jnp.where(lo_lanes, xp, jnp.zeros_like(xp)) if half == 0 else jnp.where(
                    lo_lanes, jnp.zeros_like(xp), xp)
                t = _dot(w, xm)
                yd = t if yd is None else yd + t
            y_parts.append(yd)
        h_prev = h_s[g]
        y_off = _dot(cm, h_prev.astype(BF16)) * ecs_x[:, g * gw:(g + 1) * gw]
        y_parts.append(y_off)
        st_new = _dot_tn(bm, x_st[:, g * gw:(g + 1) * gw])
        h_s[g] = h_prev * ecs_x[L - 1:L, g * gw:(g + 1) * gw] + st_new

    per_g = 1 + hpg // 2
    outs = []
    zg = z_ref[rows, :].astype(F32)
    for g in range(SSD_GROUPS):
        yd = jnp.concatenate(y_parts[g * per_g:g * per_g + hpg // 2], axis=1)
        y = yd + y_parts[g * per_g + hpg // 2] + dexp_ref[:, g * gw:(g + 1) * gw] * xs[:, g * gw:(g + 1) * gw]
        zz = zg[:, g * gw:(g + 1) * gw]
        y = y * _silu(zz)
        ms = jnp.mean(y * y, axis=-1, keepdims=True)
        outs.append(y * lax.rsqrt(ms + NORM_EPS) * nw_ref[:, g * gw:(g + 1) * gw])
    o_ref[rows, :] = jnp.concatenate(outs, axis=1).astype(BF16)


def _gla_phases(q_ref, k_ref, v_ref, z_ref, sm_ref, gw_ref, gb_ref, nw_ref, tri_ref, sel_ref,
                msk_ref, o_ref, st_s, att_s):
    L = CHUNK
    kk = GLA_HEAD_K
    vv = GLA_HEAD_V

    lane = lax.broadcasted_iota(jnp.int32, (L, GLA_KEY), 1)
    rowi = lax.broadcasted_iota(jnp.int32, (L, GLA_KEY), 0)
    head_of_lane = lane // kk

    def head_mask(xb, h):
        return jnp.where(head_of_lane == h, xb, jnp.zeros_like(xb))

    g_hi, g_lo = _split2(gw_ref[...])
    tri = tri_ref[...]
    subs = []
    for sub in range(MIX_SUBCHUNKS):
        rows = slice(sub * L, (sub + 1) * L)
        s_hi, s_lo = _split2(sm_ref[rows, :])
        u = _dot(s_hi, g_hi) + _dot(s_lo, g_hi) + _dot(s_hi, g_lo) + gb_ref[...]
        lg = _log_sigmoid(u) * (1.0 / GLA_GATE_TAU)
        l1, l2, l3 = _split3(lg)
        gc = _dot(tri, l1) + _dot(tri, l2) + _dot(tri, l3)
        q = q_ref[rows, :].astype(F32) * (kk ** -0.5)
        k = k_ref[rows, :].astype(F32)
        g_last = gc[L - 1:L, :]
        q_in = (q * jnp.exp(gc)).astype(BF16)
        k_dec = (k * jnp.exp(g_last - gc)).astype(BF16)
        k_growth = jnp.max(jnp.abs(k), axis=0, keepdims=True) * jnp.exp(-g_last)
        subs.append(dict(rows=rows, gc=gc, q=q, k=k, g_last=g_last, q_in=q_in, k_dec=k_dec, growth=k_growth))

    growth = subs[0]["growth"]
    for s in subs[1:]:
        growth = jnp.maximum(growth, s["growth"])
    direct = jnp.max(growth) < GLA_DIRECT_MAX
    yield

    @pl.when(direct)
    def _():
        m = msk_ref[GLA_LEVELS + 1] > 0.0
        for i, s in enumerate(subs):
            k_up = (s["k"] * jnp.exp(-s["gc"])).astype(BF16)
            for h in range(GLA_HEADS):
                att_s[i, h] = jnp.where(m, _dot_nt(head_mask(s["q_in"], h), k_up), 0.0)

    @pl.when(jnp.logical_not(direct))
    def _():
        sel = sel_ref[...]
        for i, s in enumerate(subs):
            gc, q, k = s["gc"], s["q"], s["k"]
            c1, c2, c3 = _split3(gc)
            ref_g = _dot(sel, c1) + _dot(sel, c2) + _dot(sel, c3)
            att = [None] * GLA_HEADS
            for lv in range(GLA_LEVELS):
                expo = -jnp.abs(gc - ref_g[lv * L:(lv + 1) * L, :])
                upper = ((rowi >> lv) & 1) == 1
                xb = (jnp.where(upper, q, k) * jnp.exp(expo)).astype(BF16)
                m = msk_ref[lv]
                for h in range(GLA_HEADS):
                    p = _dot_nt(head_mask(xb, h), xb) * m
                    att[h] = p if att[h] is None else att[h] + p
            qb = q.astype(BF16)
            kb = k.astype(BF16)
            m = msk_ref[GLA_LEVELS]
            for h in range(GLA_HEADS):
                att_s[i, h] = att[h] + _dot_nt(head_mask(qb, h), kb) * m

    yield
    lane_s = lax.broadcasted_iota(jnp.int32, (vv, GLA_KEY), 1) // kk
    st = st_s[...]
    for i, s in enumerate(subs):
        rows = s["rows"]
        st_b = st.astype(BF16)
        st_new = st * jnp.exp(s["g_last"])
        outs = []
        for h in range(GLA_HEADS):
            vh = v_ref[rows, h * vv:(h + 1) * vv]
            o = _dot_nt(head_mask(s["q_in"], h), st_b) + _dot(att_s[i, h].astype(BF16), vh)
            upd = _dot_tn(vh, s["k_dec"])
            st_new = st_new + jnp.where(lane_s == h, upd, 0.0)
            ms = jnp.mean(o * o, axis=-1, keepdims=True)
            o = o * lax.rsqrt(ms + NORM_EPS) * nw_ref[...]
            zz = z_ref[rows, h * vv:(h + 1) * vv].astype(F32)
            outs.append(o * _silu(zz))
        o_ref[rows, :] = jnp.concatenate(outs, axis=1).astype(BF16)
        st = st_new
    st_s[...] = st


def _ssd_gla(proj, small, cw, cb, dtb, alog, dexp, snw, e16, tri, shift, gw, gb, gnw, sel, msk, batch, seq):
    step = MIX_SUBCHUNKS * CHUNK
    nc = seq // step
    row = lambda b, c: b * nc + c
    prev_row = lambda b, c: b * (seq // CHUNK) + jnp.maximum(c * MIX_SUBCHUNKS - 1, 0)
    const = lambda shape: pl.BlockSpec(shape, lambda b, c: (0,) * len(shape))
    rows = lambda width, off: pl.BlockSpec((step, width), lambda b, c: (row(b, c), off // width))
    return pl.pallas_call(
        _mix_kernel,
        grid=(batch, nc),
        in_specs=[
            rows(SSD_WIDTH, OFF_SSD_Z), rows(SSD_CONV_DIM, OFF_SSD_XBC),
            pl.BlockSpec((CHUNK, SSD_CONV_DIM), lambda b, c: (prev_row(b, c), OFF_SSD_XBC // SSD_CONV_DIM)),
            rows(LANES, 0),
            const((SSD_CONV, SSD_CONV_DIM)), const((1, SSD_CONV_DIM)), const((1, LANES)), const((1, LANES)),
            const((1, SSD_WIDTH)), const((1, SSD_WIDTH)), const((LANES, SSD_WIDTH)), const((CHUNK, CHUNK)),
            const(((SSD_CONV - 1) * CHUNK, 2 * CHUNK)),
            rows(GLA_KEY, OFF_GLA_Q), rows(GLA_KEY, OFF_GLA_K), rows(GLA_WIDTH, OFF_GLA_V), rows(GLA_WIDTH, OFF_GLA_Z),
            const((LANES, GLA_KEY)), const((1, GLA_KEY)), const((1, GLA_HEAD_V)),
            const((GLA_LEVELS * CHUNK, CHUNK)), const((GLA_LEVELS + 2, CHUNK, CHUNK)),
        ],
        out_specs=[pl.BlockSpec((step, SSD_WIDTH), lambda b, c: (row(b, c), 0)),
                   pl.BlockSpec((step, GLA_WIDTH), lambda b, c: (row(b, c), 0))],
        out_shape=[jax.ShapeDtypeStruct((batch * seq, SSD_WIDTH), BF16),
                   jax.ShapeDtypeStruct((batch * seq, GLA_WIDTH), BF16)],
        scratch_shapes=[pltpu.VMEM((SSD_GROUPS, SSD_STATE, SSD_WIDTH // SSD_GROUPS), F32),
                        pltpu.VMEM((GLA_HEAD_V, GLA_KEY), F32),
                        pltpu.VMEM((MIX_SUBCHUNKS, GLA_HEADS, CHUNK, CHUNK), F32)],
        compiler_params=pltpu.CompilerParams(
            dimension_semantics=("arbitrary", "arbitrary"), vmem_limit_bytes=VMEM_LIMIT),
        name="ssd_gla",
    )(proj, proj, proj, small, cw, cb, dtb, alog, dexp, snw, e16, tri, shift,
      proj, proj, proj, proj, gw, gb, gnw, sel, msk)


def _outproj_kernel(x_ref, ya_ref, yb_ref, yc_ref, w_ref, fw_ref, o_ref, *, final):
    y = _dot(ya_ref[...], w_ref[0:SB_WIDTH, :])
    y = y + _dot(yb_ref[...], w_ref[SB_WIDTH:SB_WIDTH + SSD_WIDTH, :])
    y = y + _dot(yc_ref[...], w_ref[SB_WIDTH + SSD_WIDTH:D_INNER, :])
    xn = x_ref[...] + y
    if final:
        ms = jnp.mean(xn * xn, axis=-1, keepdims=True)
        xn = xn * lax.rsqrt(ms + NORM_EPS) * fw_ref[...]
    o_ref[...] = xn


def _outproj(x2, ya, yb, yc, w, fw, *, final, tm=512):
    t = x2.shape[0]
    rows = lambda width: pl.BlockSpec((tm, width), lambda i: (i, 0))
    return pl.pallas_call(
        functools.partial(_outproj_kernel, final=final),
        grid=(t // tm,),
        in_specs=[rows(D_MODEL), rows(SB_WIDTH), rows(SSD_WIDTH), rows(GLA_WIDTH),
                  pl.BlockSpec((D_INNER, D_MODEL), lambda i: (0, 0)),
                  pl.BlockSpec((1, D_MODEL), lambda i: (0, 0))],
        out_specs=rows(D_MODEL),
        out_shape=jax.ShapeDtypeStruct((t, D_MODEL), F32),
        compiler_params=pltpu.CompilerParams(
            dimension_semantics=("arbitrary",), vmem_limit_bytes=VMEM_LIMIT),
        name="outproj_final" if final else "outproj",
    )(x2, ya, yb, yc, w, fw)


def _constants():
    L = CHUNK
    t = np.arange(L)
    tri = (t[:, None] >= t[None, :]).astype(np.float32)
    sel = np.zeros((GLA_LEVELS * L, L), np.float32)
    msk = np.zeros((GLA_LEVELS + 2, L, L), np.float32)
    msk[GLA_LEVELS + 1] = tri
    shift = np.zeros(((SSD_CONV - 1) * L, 2 * L), np.float32)
    for j in range(SSD_CONV - 1):
        shift[j * L + t, L + t - (SSD_CONV - 1 - j)] = 1.0
    for lv in range(GLA_LEVELS):
        mid = ((t >> (lv + 1)) << (lv + 1)) + (1 << lv)
        sel[lv * L + t, mid - 1] = 1.0
        same = (t[:, None] >> (lv + 1)) == (t[None, :] >> (lv + 1))
        up = ((t[:, None] >> lv) & 1) == 1
        low = ((t[None, :] >> lv) & 1) == 0
        msk[lv] = (same & up & low).astype(np.float32)
    msk[GLA_LEVELS] = np.eye(L, dtype=np.float32)
    e16 = np.zeros((LANES, SSD_WIDTH), np.float32)
    for h in range(SSD_HEADS):
        e16[h, h * SSD_HEAD_DIM:(h + 1) * SSD_HEAD_DIM] = 1.0
    return (jnp.asarray(tri, BF16), jnp.asarray(sel, BF16), jnp.asarray(msk, F32), jnp.asarray(e16, BF16),
            jnp.asarray(shift, BF16))


def _pad_lanes(v, offset=0):
    out = jnp.zeros((1, LANES), F32)
    return out.at[0, offset:offset + v.shape[0]].set(v.astype(F32))


def kernel(x, norm_w, w_in, ssd_conv_w, ssd_conv_b, ssd_dt_bias, ssd_a_log, ssd_d, ssd_norm_w,
           gla_gate_w, gla_gate_b, gla_norm_w, w_out, final_norm_w):
    batch, seq, _ = x.shape
    assert seq % 256 == 0 and (batch * seq) % 512 == 0
    tri, sel, msk, e16, shift = _constants()
    x2 = x.reshape(batch * seq, D_MODEL).astype(F32)
    dt_col = OFF_GLA_Q
    glr_col = dt_col + SSD_HEADS + 2 * GLA_KEY + 2 * GLA_WIDTH
    for layer in range(DEPTH):
        w = w_in[layer]
        wa = w[:, :dt_col].astype(BF16)
        wb = w[:, dt_col + SSD_HEADS:glr_col].astype(BF16)
        ws = jnp.concatenate(
            [w[:, dt_col:dt_col + SSD_HEADS], w[:, glr_col:glr_col + GLA_GATE_RANK],
             jnp.zeros((D_MODEL, LANES - SSD_HEADS - GLA_GATE_RANK), w.dtype)], axis=1).astype(BF16)
        proj, small = _inproj(x2, norm_w[layer].reshape(1, D_MODEL), wa, wb, ws)

        ya = _sb_attention(proj, batch, seq)
        gw = jnp.zeros((LANES, GLA_KEY), F32).at[SSD_HEADS:SSD_HEADS + GLA_GATE_RANK, :].set(gla_gate_w[layer])
        yb, yc = _ssd_gla(proj, small, ssd_conv_w[layer], ssd_conv_b[layer].reshape(1, SSD_CONV_DIM),
                          _pad_lanes(ssd_dt_bias[layer]), _pad_lanes(ssd_a_log[layer]),
                          jnp.repeat(ssd_d[layer], SSD_HEAD_DIM).reshape(1, SSD_WIDTH),
                          ssd_norm_w[layer].reshape(1, SSD_WIDTH), e16, tri, shift,
                          gw, gla_gate_b[layer].reshape(1, GLA_KEY), gla_norm_w[layer].reshape(1, GLA_HEAD_V),
                          sel, msk, batch, seq)
        x2 = _outproj(x2, ya, yb, yc, w_out[layer].astype(BF16), final_norm_w.reshape(1, D_MODEL),
                      final=(layer == DEPTH - 1))
    return x2.reshape(batch, seq, D_MODEL).astype(x.dtype)
```

```python
import functools

import jax
import jax.numpy as jnp
import numpy as np
from jax import lax
from jax.experimental import pallas as pl
from jax.experimental.pallas import tpu as pltpu

F32 = jnp.float32
BF16 = jnp.bfloat16

D_MODEL = 1024
DEPTH = 2
D_INNER = 2 * D_MODEL
SB_WIDTH = D_INNER // 4
SB_HEAD_DIM = 64
SSD_WIDTH = D_INNER // 2
SSD_HEAD_DIM = 64
SSD_HEADS = SSD_WIDTH // SSD_HEAD_DIM
SSD_GROUPS = 2
SSD_STATE = 128
SSD_CONV = 4
SSD_CONV_DIM = SSD_WIDTH + 2 * SSD_GROUPS * SSD_STATE
GLA_WIDTH = D_INNER // 4
GLA_HEADS = 4
GLA_KEY = GLA_WIDTH // 2
GLA_HEAD_K = GLA_KEY // GLA_HEADS
GLA_HEAD_V = GLA_WIDTH // GLA_HEADS
GLA_GATE_RANK = 16
GLA_GATE_TAU = 16.0
NORM_EPS = 1e-6

LANES = 128
SUBLANES = 8
CHUNK = 128
MIX_SUBCHUNKS = 4
N_MAIN = 4 * SB_WIDTH + SSD_WIDTH + SSD_CONV_DIM + 2 * GLA_KEY + 2 * GLA_WIDTH
OFF_SSD_Z = 4 * SB_WIDTH
OFF_SSD_XBC = OFF_SSD_Z + SSD_WIDTH
OFF_GLA_Q = OFF_SSD_XBC + SSD_CONV_DIM
OFF_GLA_K = OFF_GLA_Q + GLA_KEY
OFF_GLA_V = OFF_GLA_K + GLA_KEY
OFF_GLA_Z = OFF_GLA_V + GLA_WIDTH
GLA_LEVELS = 7
GLA_DIRECT_MAX = 1e30
SB_TQ = 512
SB_NEAR_EXTRA = 2
LOG2E = 1.4426950408889634
VMEM_LIMIT = 56 * 1024 * 1024


def _dot(a, b):
    return jnp.dot(a, b, preferred_element_type=F32)


def _dot_nt(a, b):
    return lax.dot_general(a, b, (((1,), (1,)), ((), ())), preferred_element_type=F32)


def _dot_tn(a, b):
    return lax.dot_general(a, b, (((0,), (0,)), ((), ())), preferred_element_type=F32)


def _split2(x):
    hi = x.astype(BF16)
    lo = (x - hi.astype(F32)).astype(BF16)
    return hi, lo


def _split3(x):
    hi = x.astype(BF16)
    r = x - hi.astype(F32)
    mid = r.astype(BF16)
    lo = (r - mid.astype(F32)).astype(BF16)
    return hi, mid, lo


def _silu(x):
    hx = 0.5 * x
    return hx * jnp.tanh(hx) + hx


def _log_sigmoid(x):
    return jnp.minimum(x, 0.0) - jnp.log(1.0 + jnp.exp(-jnp.abs(x)))


def _softplus(x):
    return jnp.maximum(x, 0.0) + jnp.log(1.0 + jnp.exp(-jnp.abs(x)))


def _inproj_kernel(x_ref, nw_ref, wa_ref, wb_ref, ws_ref, o_ref, os_ref, *, n_chunk):
    x = x_ref[...]
    ms = jnp.mean(x * x, axis=-1, keepdims=True)
    h = (x * lax.rsqrt(ms + NORM_EPS) * nw_ref[...]).astype(BF16)
    for c0 in range(0, OFF_GLA_Q, n_chunk):
        o_ref[:, c0:c0 + n_chunk] = _dot(h, wa_ref[:, c0:c0 + n_chunk]).astype(BF16)
    for c0 in range(0, N_MAIN - OFF_GLA_Q, n_chunk):
        o_ref[:, OFF_GLA_Q + c0:OFF_GLA_Q + c0 + n_chunk] = _dot(h, wb_ref[:, c0:c0 + n_chunk]).astype(BF16)
    os_ref[...] = _dot(h, ws_ref[...])


def _inproj(x2, nw, wa, wb, ws, *, tm=512, n_chunk=1536):
    t = x2.shape[0]
    wspec = lambda cols: pl.BlockSpec((D_MODEL, cols), lambda i: (0, 0))
    return pl.pallas_call(
        functools.partial(_inproj_kernel, n_chunk=n_chunk),
        grid=(t // tm,),
        in_specs=[
            pl.BlockSpec((tm, D_MODEL), lambda i: (i, 0)),
            pl.BlockSpec((1, D_MODEL), lambda i: (0, 0)),
            wspec(OFF_GLA_Q), wspec(N_MAIN - OFF_GLA_Q), wspec(LANES),
        ],
        out_specs=[
            pl.BlockSpec((tm, N_MAIN), lambda i: (i, 0)),
            pl.BlockSpec((tm, LANES), lambda i: (i, 0)),
        ],
        out_shape=[
            jax.ShapeDtypeStruct((t, N_MAIN), BF16),
            jax.ShapeDtypeStruct((t, LANES), F32),
        ],
        compiler_params=pltpu.CompilerParams(
            dimension_semantics=("arbitrary",), vmem_limit_bytes=VMEM_LIMIT),
        name="inproj",
    )(x2, nw, wa, wb, ws)


def _sb_kernel(*refs, seq, far):
    if far:
        q_ref, k_ref, v_ref, z_ref, yin_ref, car_ref, o_ref = refs[:7]
    else:
        q_ref, k_ref, v_ref, z_ref, o_ref, car_ref = refs[:6]
    k0_s, k1_s, vt_s, acc_s, zt0_s, zt1_s, a0_s, a1_s, qs_s = refs[-9:]
    zt_s = (zt0_s, zt1_s)
    a_s = (a0_s, a1_s)
    kb_n = LANES
    n_kb = seq // kb_n
    groups = kb_n // SUBLANES
    hd = SB_HEAD_DIM
    tq = SB_TQ
    U = tq // kb_n

    lane = lax.broadcasted_iota(jnp.int32, (kb_n, kb_n), 1)
    row = lax.broadcasted_iota(jnp.int32, (kb_n, kb_n), 0)
    src = (row % SUBLANES) * groups + row // SUBLANES
    perm = jnp.where(lane == src, 1.0, 0.0).astype(BF16)
    src_t = (lane % SUBLANES) * groups + lane // SUBLANES
    perm_t = jnp.where(row == src_t, 1.0, 0.0).astype(BF16)
    lo_lanes = lane < hd

    def prep(n, carry):
        for j in range(U):
            kb = n * U + j
            r0 = pl.multiple_of(kb * kb_n, kb_n)
            kp = _dot(perm, k_ref[pl.ds(r0, kb_n), :]).astype(BF16)
            k0_s[kb] = jnp.where(lo_lanes, kp, jnp.zeros_like(kp))
            k1_s[kb] = jnp.where(lo_lanes, jnp.zeros_like(kp), kp)
            vt = v_ref[pl.ds(r0, kb_n), :].astype(F32).T.astype(BF16)
            vt_s[kb] = _dot(vt, perm_t).astype(BF16)
            qs_s[pl.ds(r0, kb_n), :] = (q_ref[pl.ds(r0, kb_n), :].astype(F32) * (0.5 * hd ** -0.5)).astype(BF16)
        return carry

    lax.fori_loop(0, n_kb // U, prep, 0)

    def block_scan(zt, base):
        nl = zt.shape[1]
        seg = lax.broadcasted_iota(jnp.int32, (SUBLANES, nl), 0)
        run = jnp.ones((SUBLANES, nl), F32)
        s_rows = [None] * groups
        for i in reversed(range(groups)):
            zh = zt[i * SUBLANES:(i + 1) * SUBLANES, :]
            beta = 0.5 * jnp.tanh(zh) + 0.5
            if base is not None:
                beta = jnp.where(i < base, beta, 0.0)
            s_rows[i] = beta * run
            run = run - s_rows[i]
        inc = run
        for sh in (1, 2, 4):
            inc = inc * jnp.where(seg < SUBLANES - sh, pltpu.roll(inc, SUBLANES - sh, 0), 1.0)
        excl = jnp.where(seg < SUBLANES - 1, pltpu.roll(inc, SUBLANES - 1, 0), 1.0)
        total = jnp.broadcast_to(inc[0:1, :], (SUBLANES, nl))
        return s_rows, excl, total

    def chunk(ztc, base, cc):
        s_rows, excl, total = block_scan(ztc, base)
        off = excl * cc
        a = jnp.concatenate([s_rows[i] * off for i in range(groups)], axis=0)
        return a, cc * total

    seg_d = lax.broadcasted_iota(jnp.int32, (SUBLANES, kb_n), 0)
    lane_d = lax.broadcasted_iota(jnp.int32, (SUBLANES, kb_n), 1)
    base_d = lane_d - seg_d * groups

    n_q = seq // tq

    def issue_scores(item, slot):
        qi, kb0, _, nb = item
        qb = qs_s[qi * tq:(qi + 1) * tq, :]
        k_cat = jnp.concatenate([k0_s[kb0 + j] for j in range(nb)] + [k1_s[kb0 + j] for j in range(nb)], axis=0)
        zt_s[slot][0:2 * nb * kb_n, :] = _dot_nt(k_cat, qb)

    def add_values(item, slot):
        _, kb0, _, nb = item
        vts = [vt_s[kb0 + j] for j in range(nb)]
        for h in range(2):
            v_cat = jnp.concatenate([vts[j][h * hd:(h + 1) * hd, :] for j in range(nb)], axis=1)
            acc_s[h] += _dot(v_cat, a_s[slot][h, 0:nb * kb_n, :])

    def finish_qblock(qi):
        q0 = qi * tq
        o = jnp.concatenate([acc_s[0], acc_s[1]], axis=0).T
        zg = z_ref[pl.ds(q0, tq), :].astype(F32)
        y = o * _silu(zg)
        if far:
            y = y + yin_ref[pl.ds(q0, tq), :].astype(F32)
        o_ref[pl.ds(q0, tq), :] = y.astype(BF16)
        acc_s[...] = jnp.zeros_like(acc_s)

    def weights(slot, carries, item):
        _, _, diag, nb = item
        out = []
        for h in range(2):
            carry = carries[h]
            for j in reversed(range(nb)):
                r0 = (h * nb + j) * kb_n
                if not diag:
                    pieces = [chunk(zt_s[slot][r0:r0 + kb_n, l:l + kb_n], None, carry[:, l:l + kb_n])
                              for l in range(0, tq, kb_n)]
                    a = jnp.concatenate([p[0] for p in pieces], axis=1)
                    carry = jnp.concatenate([p[1] for p in pieces], axis=1)
                else:
                    l0 = j * kb_n
                    parts, cparts = [], []
                    if l0 > 0:
                        parts.append(jnp.zeros((kb_n, l0), F32))
                        cparts.append(carry[:, 0:l0])
                    a_m, c_m = chunk(zt_s[slot][r0:r0 + kb_n, l0:l0 + kb_n], base_d, carry[:, l0:l0 + kb_n])
                    parts.append(a_m)
                    cparts.append(c_m)
                    if l0 + kb_n < tq:
                        a_u, c_u = chunk(zt_s[slot][r0:r0 + kb_n, l0 + kb_n:tq], None, carry[:, l0 + kb_n:tq])
                        parts.append(a_u)
                        cparts.append(c_u)
                    a = jnp.concatenate(parts, axis=1)
                    carry = jnp.concatenate(cparts, axis=1)
                a_s[slot][h, j * kb_n:(j + 1) * kb_n, :] = a.astype(BF16)
            out.append(carry)
        return tuple(out)

    items = []
    for qi in range(n_q):
        if far:
            if qi > 0:
                items.append((qi, (qi - 1) * U, False, U - SB_NEAR_EXTRA))
            items.extend((qi, j * U, False, U) for j in reversed(range(qi - 1)))
        else:
            items.append((qi, qi * U, True, U))
            if qi > 0:
                items.append((qi, qi * U - SB_NEAR_EXTRA, False, SB_NEAR_EXTRA))
    first_item = {}
    for n, it in enumerate(items):
        first_item.setdefault(it[0], n)
    if far:
        for qi in range(n_q):
            if qi not in first_item:
                q0 = qi * tq
                o_ref[pl.ds(q0, tq), :] = yin_ref[pl.ds(q0, tq), :]

    acc_s[...] = jnp.zeros_like(acc_s)
    issue_scores(items[0], 0)
    carries = None
    for n, item in enumerate(items):
        qi = item[0]
        if n + 1 < len(items):
            issue_scores(items[n + 1], (n + 1) % 2)
        if n > 0:
            add_values(items[n - 1], (n - 1) % 2)
        if first_item[qi] == n:
            if n > 0:
                finish_qblock(items[n - 1][0])
            if far:
                carries = (car_ref[qi, 0], car_ref[qi, 1])
            else:
                one = jnp.ones((SUBLANES, tq), F32)
                carries = (one, one)
        carries = weights(n % 2, carries, item)
        if not far and (n + 1 == len(items) or items[n + 1][0] != qi):
            car_ref[qi, 0] = carries[0]
            car_ref[qi, 1] = carries[1]
    add_values(items[-1], (len(items) - 1) % 2)
    finish_qblock(items[-1][0])


def _sb_call(proj, batch, seq, far, extra):
    pairs = SB_WIDTH // LANES
    n_kb = seq // LANES
    n_q = seq // SB_TQ
    blk = lambda off: pl.BlockSpec((seq, LANES), lambda b, p, off=off: (b, off + p))
    y_spec = pl.BlockSpec((seq, LANES), lambda b, p: (b, p))
    car_spec = pl.BlockSpec((None, None, n_q, 2, SUBLANES, SB_TQ), lambda b, p: (b, p, 0, 0, 0, 0))
    y_shape = jax.ShapeDtypeStruct((batch * seq, SB_WIDTH), BF16)
    car_shape = jax.ShapeDtypeStruct((batch, pairs, n_q, 2, SUBLANES, SB_TQ), F32)
    return pl.pallas_call(
        functools.partial(_sb_kernel, seq=seq, far=far),
        grid=(batch, pairs),
        in_specs=[blk(0), blk(pairs), blk(2 * pairs), blk(3 * pairs)] + ([y_spec, car_spec] if far else []),
        out_specs=y_spec if far else [y_spec, car_spec],
        out_shape=y_shape if far else [y_shape, car_shape],
        scratch_shapes=[
            pltpu.VMEM((n_kb, LANES, LANES), BF16),
            pltpu.VMEM((n_kb, LANES, LANES), BF16),
            pltpu.VMEM((n_kb, LANES, LANES), BF16),
            pltpu.VMEM((2, SB_HEAD_DIM, SB_TQ), F32),
            pltpu.VMEM((2 * SB_TQ, SB_TQ), F32),
            pltpu.VMEM((2 * SB_TQ, SB_TQ), F32),
            pltpu.VMEM((2, SB_TQ, SB_TQ), BF16),
            pltpu.VMEM((2, SB_TQ, SB_TQ), BF16),
            pltpu.VMEM((seq, LANES), BF16),
        ],
        compiler_params=pltpu.CompilerParams(
            dimension_semantics=("arbitrary", "arbitrary"), vmem_limit_bytes=VMEM_LIMIT),
        name="sb_far" if far else "sb_near",
    )(proj, proj, proj, proj, *extra)


def _sb_attention(proj, batch, seq):
    ya, car = _sb_call(proj, batch, seq, False, ())
    if seq // SB_TQ <= 1:
        return ya
    alive = jnp.max(car[:, :, 1:]) > 0.0
    return lax.cond(alive, lambda: _sb_call(proj, batch, seq, True, (ya, car)), lambda: ya)


def _mix_kernel(z_ref, xbc_ref, xprev_ref, sm_ref, cw_ref, cb_ref, dtb_ref, alog_ref, dexp_ref, snw_ref,
                e16_ref, tri_ref, shift_ref, q_ref, k_ref, v_ref, gz_ref, gw_ref, gb_ref, gnw_ref,
                sel_ref, msk_ref, ob_ref, oc_ref, h_s, st_s, att_s):
    first = pl.program_id(1) == 0

    @pl.when(first)
    def _():
        h_s[...] = jnp.zeros_like(h_s)
        st_s[...] = jnp.zeros_like(st_s)

    gla = _gla_phases(q_ref, k_ref, v_ref, gz_ref, sm_ref, gw_ref, gb_ref, gnw_ref, tri_ref, sel_ref,
                      msk_ref, oc_ref, st_s, att_s)
    next(gla)
    staged = [_ssd_stage1(sub, first, xbc_ref, xprev_ref, sm_ref, cw_ref, cb_ref, dtb_ref, alog_ref,
                          e16_ref, tri_ref, shift_ref) for sub in range(MIX_SUBCHUNKS)]
    next(gla)
    for sub in range(MIX_SUBCHUNKS):
        _ssd_stage2(sub, staged[sub], z_ref, dexp_ref, snw_ref, ob_ref, h_s)
    for _ in gla:
        pass


def _ssd_stage1(sub, first, xbc_ref, xprev_ref, sm_ref, cw_ref, cb_ref, dtb_ref, alog_ref,
                e16_ref, tri_ref, shift_ref):
    L = CHUNK
    rows = slice(sub * L, (sub + 1) * L)

    cur = xbc_ref[rows, :]
    if sub == 0:
        prev = xprev_ref[...]
        prev = jnp.where(first, jnp.zeros_like(prev), prev)
    else:
        prev = xbc_ref[(sub - 1) * L:sub * L, :]
    both = jnp.concatenate([prev, cur], axis=0)
    shifted = _dot(shift_ref[...], both)
    acc = cb_ref[...] + cw_ref[SSD_CONV - 1:SSD_CONV, :] * cur.astype(F32)
    for j in range(SSD_CONV - 1):
        acc = acc + cw_ref[j:j + 1, :] * shifted[j * L:(j + 1) * L, :]
    xbc = _silu(acc)
    xs = xbc[:, 0:SSD_WIDTH]

    lane = lax.broadcasted_iota(jnp.int32, (L, LANES), 1)
    head_lanes = lane < SSD_HEADS
    dt = jnp.where(head_lanes, _softplus(sm_ref[rows, :] + dtb_ref[...]), 0.0)
    a = dt * (-jnp.exp(alog_ref[...]))
    tri = tri_ref[...]
    a1, a2, a3 = _split3(a)
    cs = _dot(tri, a1) + _dot(tri, a2) + _dot(tri, a3)
    ecs = jnp.exp(cs)
    dend = jnp.exp(cs[L - 1:L, :] - cs)
    stack = jnp.concatenate([dt, dt * dend, ecs], axis=0)
    s_hi, s_lo = _split2(stack)
    e16 = e16_ref[...]
    ex = _dot(s_hi, e16) + _dot(s_lo, e16)
    x_dt = (xs * ex[0:L, :]).astype(BF16)
    x_st = (xs * ex[L:2 * L, :]).astype(BF16)
    ecs_x = ex[2 * L:3 * L, :]
    cs2 = cs * LOG2E
    return xbc, xs, cs2, cs2.T, x_dt, x_st, ecs_x


def _ssd_stage2(sub, staged, z_ref, dexp_ref, nw_ref, o_ref, h_s):
    xbc, xs, cs, cs_t, x_dt, x_st, ecs_x = staged
    L = CHUNK
    gw = SSD_WIDTH // SSD_GROUPS
    hpg = SSD_HEADS // SSD_GROUPS
    n = SSD_STATE
    rows = slice(sub * L, (sub + 1) * L)

    rowi = lax.broadcasted_iota(jnp.int32, (L, L), 0)
    coli = lax.broadcasted_iota(jnp.int32, (L, L), 1)
    causal = rowi >= coli
    lo_lanes = coli < SSD_HEAD_DIM

    y_parts = []
    for g in range(SSD_GROUPS):
        bm = xbc[:, SSD_WIDTH + g * n:SSD_WIDTH + (g + 1) * n].astype(BF16)
        cm = xbc[:, SSD_WIDTH + SSD_GROUPS * n + g * n:SSD_WIDTH + SSD_GROUPS * n + (g + 1) * n].astype(BF16)
        scores = _dot_nt(cm, bm)
        for pr in range(hpg // 2):
            col = g * gw + pr * LANES
            xp = x_dt[:, col:col + LANES]
            yd = None
            for half in range(2):
                h = g * hpg + 2 * pr + half
                sg = cs[:, h:h + 1] - cs_t[h:h + 1, :]
                dec = jnp.where(causal, jnp.exp2(sg), 0.0)
                w = (scores * dec).astype(BF16)
                xm = jnp.where(lo_lanes, xp, jnp.zeros_like(xp)) if half == 0 else jnp.where(
                    lo_lanes, jnp.zeros_like(xp), xp)
                t = _dot(w, xm)
                yd = t if yd is None else yd + t
            y_parts.append(yd)
        h_prev = h_s[g]
        y_off = _dot(cm, h_prev.astype(BF16)) * ecs_x[:, g * gw:(g + 1) * gw]
        y_parts.append(y_off)
        st_new = _dot_tn(bm, x_st[:, g * gw:(g + 1) * gw])
        h_s[g] = h_prev * ecs_x[L - 1:L, g * gw:(g + 1) * gw] + st_new

    per_g = 1 + hpg // 2
    outs = []
    zg = z_ref[rows, :].astype(F32)
    for g in range(SSD_GROUPS):
        yd = jnp.concatenate(y_parts[g * per_g:g * per_g + hpg // 2], axis=1)
        y = yd + y_parts[g * per_g + hpg // 2] + dexp_ref[:, g * gw:(g + 1) * gw] * xs[:, g * gw:(g + 1) * gw]
        zz = zg[:, g * gw:(g + 1) * gw]
        y = y * _silu(zz)
        ms = jnp.mean(y * y, axis=-1, keepdims=True)
        outs.append(y * lax.rsqrt(ms + NORM_EPS) * nw_ref[:, g * gw:(g + 1) * gw])
    o_ref[rows, :] = jnp.concatenate(outs, axis=1).astype(BF16)


def _gla_phases(q_ref, k_ref, v_ref, z_ref, sm_ref, gw_ref, gb_ref, nw_ref, tri_ref, sel_ref,
                msk_ref, o_ref, st_s, att_s):
    L = CHUNK
    kk = GLA_HEAD_K
    vv = GLA_HEAD_V

    lane = lax.broadcasted_iota(jnp.int32, (L, GLA_KEY), 1)
    rowi = lax.broadcasted_iota(jnp.int32, (L, GLA_KEY), 0)
    head_of_lane = lane // kk

    def head_mask(xb, h):
        return jnp.where(head_of_lane == h, xb, jnp.zeros_like(xb))

    g_hi, g_lo = _split2(gw_ref[...])
    tri = tri_ref[...]
    subs = []
    for sub in range(MIX_SUBCHUNKS):
        rows = slice(sub * L, (sub + 1) * L)
        s_hi, s_lo = _split2(sm_ref[rows, :])
        u = _dot(s_hi, g_hi) + _dot(s_lo, g_hi) + _dot(s_hi, g_lo) + gb_ref[...]
        lg = _log_sigmoid(u) * (1.0 / GLA_GATE_TAU)
        l1, l2, l3 = _split3(lg)
        gc = _dot(tri, l1) + _dot(tri, l2) + _dot(tri, l3)
        q = q_ref[rows, :].astype(F32) * (kk ** -0.5)
        k = k_ref[rows, :].astype(F32)
        g_last = gc[L - 1:L, :]
        q_in = (q * jnp.exp(gc)).astype(BF16)
        k_dec = (k * jnp.exp(g_last - gc)).astype(BF16)
        k_growth = jnp.max(jnp.abs(k), axis=0, keepdims=True) * jnp.exp(-g_last)
        subs.append(dict(rows=rows, gc=gc, q=q, k=k, g_last=g_last, q_in=q_in, k_dec=k_dec, growth=k_growth))

    growth = subs[0]["growth"]
    for s in subs[1:]:
        growth = jnp.maximum(growth, s["growth"])
    direct = jnp.max(growth) < GLA_DIRECT_MAX
    yield

    @pl.when(direct)
    def _():
        m = msk_ref[GLA_LEVELS + 1] > 0.0
        for i, s in enumerate(subs):
            k_up = (s["k"] * jnp.exp(-s["gc"])).astype(BF16)
            for h in range(GLA_HEADS):
                att_s[i, h] = jnp.where(m, _dot_nt(head_mask(s["q_in"], h), k_up), 0.0)

    @pl.when(jnp.logical_not(direct))
    def _():
        sel = sel_ref[...]
        for i, s in enumerate(subs):
            gc, q, k = s["gc"], s["q"], s["k"]
            c1, c2, c3 = _split3(gc)
            ref_g = _dot(sel, c1) + _dot(sel, c2) + _dot(sel, c3)
            att = [None] * GLA_HEADS
            for lv in range(GLA_LEVELS):
                expo = -jnp.abs(gc - ref_g[lv * L:(lv + 1) * L, :])
                upper = ((rowi >> lv) & 1) == 1
                xb = (jnp.where(upper, q, k) * jnp.exp(expo)).astype(BF16)
                m = msk_ref[lv]
                for h in range(GLA_HEADS):
                    p = _dot_nt(head_mask(xb, h), xb) * m
                    att[h] = p if att[h] is None else att[h] + p
            qb = q.astype(BF16)
            kb = k.astype(BF16)
            m = msk_ref[GLA_LEVELS]
            for h in range(GLA_HEADS):
                att_s[i, h] = att[h] + _dot_nt(head_mask(qb, h), kb) * m

    yield
    lane_s = lax.broadcasted_iota(jnp.int32, (vv, GLA_KEY), 1) // kk
    st = st_s[...]
    for i, s in enumerate(subs):
        rows = s["rows"]
        st_b = st.astype(BF16)
        st_new = st * jnp.exp(s["g_last"])
        outs = []
        for h in range(GLA_HEADS):
            vh = v_ref[rows, h * vv:(h + 1) * vv]
            o = _dot_nt(head_mask(s["q_in"], h), st_b) + _dot(att_s[i, h].astype(BF16), vh)
            upd = _dot_tn(vh, s["k_dec"])
            st_new = st_new + jnp.where(lane_s == h, upd, 0.0)
            ms = jnp.mean(o * o, axis=-1, keepdims=True)
            o = o * lax.rsqrt(ms + NORM_EPS) * nw_ref[...]
            zz = z_ref[rows, h * vv:(h + 1) * vv].astype(F32)
            outs.append(o * _silu(zz))
        o_ref[rows, :] = jnp.concatenate(outs, axis=1).astype(BF16)
        st = st_new
    st_s[...] = st


def _ssd_gla(proj, small, cw, cb, dtb, alog, dexp, snw, e16, tri, shift, gw, gb, gnw, sel, msk, batch, seq):
    step = MIX_SUBCHUNKS * CHUNK
    nc = seq // step
    row = lambda b, c: b * nc + c
    prev_row = lambda b, c: b * (seq // CHUNK) + jnp.maximum(c * MIX_SUBCHUNKS - 1, 0)
    const = lambda shape: pl.BlockSpec(shape, lambda b, c: (0,) * len(shape))
    rows = lambda width, off: pl.BlockSpec((step, width), lambda b, c: (row(b, c), off // width))
    return pl.pallas_call(
        _mix_kernel,
        grid=(batch, nc),
        in_specs=[
            rows(SSD_WIDTH, OFF_SSD_Z), rows(SSD_CONV_DIM, OFF_SSD_XBC),
            pl.BlockSpec((CHUNK, SSD_CONV_DIM), lambda b, c: (prev_row(b, c), OFF_SSD_XBC // SSD_CONV_DIM)),
            rows(LANES, 0),
            const((SSD_CONV, SSD_CONV_DIM)), const((1, SSD_CONV_DIM)), const((1, LANES)), const((1, LANES)),
            const((1, SSD_WIDTH)), const((1, SSD_WIDTH)), const((LANES, SSD_WIDTH)), const((CHUNK, CHUNK)),
            const(((SSD_CONV - 1) * CHUNK, 2 * CHUNK)),
            rows(GLA_KEY, OFF_GLA_Q), rows(GLA_KEY, OFF_GLA_K), rows(GLA_WIDTH, OFF_GLA_V), rows(GLA_WIDTH, OFF_GLA_Z),
            const((LANES, GLA_KEY)), const((1, GLA_KEY)), const((1, GLA_HEAD_V)),
            const((GLA_LEVELS * CHUNK, CHUNK)), const((GLA_LEVELS + 2, CHUNK, CHUNK)),
        ],
        out_specs=[pl.BlockSpec((step, SSD_WIDTH), lambda b, c: (row(b, c), 0)),
                   pl.BlockSpec((step, GLA_WIDTH), lambda b, c: (row(b, c), 0))],
        out_shape=[jax.ShapeDtypeStruct((batch * seq, SSD_WIDTH), BF16),
                   jax.ShapeDtypeStruct((batch * seq, GLA_WIDTH), BF16)],
        scratch_shapes=[pltpu.VMEM((SSD_GROUPS, SSD_STATE, SSD_WIDTH // SSD_GROUPS), F32),
                        pltpu.VMEM((GLA_HEAD_V, GLA_KEY), F32),
                        pltpu.VMEM((MIX_SUBCHUNKS, GLA_HEADS, CHUNK, CHUNK), F32)],
        compiler_params=pltpu.CompilerParams(
            dimension_semantics=("arbitrary", "arbitrary"), vmem_limit_bytes=VMEM_LIMIT),
        name="ssd_gla",
    )(proj, proj, proj, small, cw, cb, dtb, alog, dexp, snw, e16, tri, shift,
      proj, proj, proj, proj, gw, gb, gnw, sel, msk)


def _outproj_kernel(x_ref, ya_ref, yb_ref, yc_ref, w_ref, fw_ref, o_ref, *, final):
    y = _dot(ya_ref[...], w_ref[0:SB_WIDTH, :])
    y = y + _dot(yb_ref[...], w_ref[SB_WIDTH:SB_WIDTH + SSD_WIDTH, :])
    y = y + _dot(yc_ref[...], w_ref[SB_WIDTH + SSD_WIDTH:D_INNER, :])
    xn = x_ref[...] + y
    if final:
        ms = jnp.mean(xn * xn, axis=-1, keepdims=True)
        xn = xn * lax.rsqrt(ms + NORM_EPS) * fw_ref[...]
    o_ref[...] = xn


def _outproj(x2, ya, yb, yc, w, fw, *, final, tm=512):
    t = x2.shape[0]
    rows = lambda width: pl.BlockSpec((tm, width), lambda i: (i, 0))
    return pl.pallas_call(
        functools.partial(_outproj_kernel, final=final),
        grid=(t // tm,),
        in_specs=[rows(D_MODEL), rows(SB_WIDTH), rows(SSD_WIDTH), rows(GLA_WIDTH),
                  pl.BlockSpec((D_INNER, D_MODEL), lambda i: (0, 0)),
                  pl.BlockSpec((1, D_MODEL), lambda i: (0, 0))],
        out_specs=rows(D_MODEL),
        out_shape=jax.ShapeDtypeStruct((t, D_MODEL), F32),
        compiler_params=pltpu.CompilerParams(
            dimension_semantics=("arbitrary",), vmem_limit_bytes=VMEM_LIMIT),
        name="outproj_final" if final else "outproj",
    )(x2, ya, yb, yc, w, fw)


def _constants():
    L = CHUNK
    t = np.arange(L)
    tri = (t[:, None] >= t[None, :]).astype(np.float32)
    sel = np.zeros((GLA_LEVELS * L, L), np.float32)
    msk = np.zeros((GLA_LEVELS + 2, L, L), np.float32)
    msk[GLA_LEVELS + 1] = tri
    shift = np.zeros(((SSD_CONV - 1) * L, 2 * L), np.float32)
    for j in range(SSD_CONV - 1):
        shift[j * L + t, L + t - (SSD_CONV - 1 - j)] = 1.0
    for lv in range(GLA_LEVELS):
        mid = ((t >> (lv + 1)) << (lv + 1)) + (1 << lv)
        sel[lv * L + t, mid - 1] = 1.0
        same = (t[:, None] >> (lv + 1)) == (t[None, :] >> (lv + 1))
        up = ((t[:, None] >> lv) & 1) == 1
        low = ((t[None, :] >> lv) & 1) == 0
        msk[lv] = (same & up & low).astype(np.float32)
    msk[GLA_LEVELS] = np.eye(L, dtype=np.float32)
    e16 = np.zeros((LANES, SSD_WIDTH), np.float32)
    for h in range(SSD_HEADS):
        e16[h, h * SSD_HEAD_DIM:(h + 1) * SSD_HEAD_DIM] = 1.0
    return (jnp.asarray(tri, BF16), jnp.asarray(sel, BF16), jnp.asarray(msk, F32), jnp.asarray(e16, BF16),
            jnp.asarray(shift, BF16))


def _pad_lanes(v, offset=0):
    out = jnp.zeros((1, LANES), F32)
    return out.at[0, offset:offset + v.shape[0]].set(v.astype(F32))


def kernel(x, norm_w, w_in, ssd_conv_w, ssd_conv_b, ssd_dt_bias, ssd_a_log, ssd_d, ssd_norm_w,
           gla_gate_w, gla_gate_b, gla_norm_w, w_out, final_norm_w):
    batch, seq, _ = x.shape
    assert seq % 256 == 0 and (batch * seq) % 512 == 0
    tri, sel, msk, e16, shift = _constants()
    x2 = x.reshape(batch * seq, D_MODEL).astype(F32)
    dt_col = OFF_GLA_Q
    glr_col = dt_col + SSD_HEADS + 2 * GLA_KEY + 2 * GLA_WIDTH
    for layer in range(DEPTH):
        w = w_in[layer]
        wa = w[:, :dt_col].astype(BF16)
        wb = w[:, dt_col + SSD_HEADS:glr_col].astype(BF16)
        ws = jnp.concatenate(
            [w[:, dt_col:dt_col + SSD_HEADS], w[:, glr_col:glr_col + GLA_GATE_RANK],
             jnp.zeros((D_MODEL, LANES - SSD_HEADS - GLA_GATE_RANK), w.dtype)], axis=1).astype(BF16)
        proj, small = _inproj(x2, norm_w[layer].reshape(1, D_MODEL), wa, wb, ws)

        ya = _sb_attention(proj, batch, seq)
        gw = jnp.zeros((LANES, GLA_KEY), F32).at[SSD_HEADS:SSD_HEADS + GLA_GATE_RANK, :].set(gla_gate_w[layer])
        yb, yc = _ssd_gla(proj, small, ssd_conv_w[layer], ssd_conv_b[layer].reshape(1, SSD_CONV_DIM),
                          _pad_lanes(ssd_dt_bias[layer]), _pad_lanes(ssd_a_log[layer]),
                          jnp.repeat(ssd_d[layer], SSD_HEAD_DIM).reshape(1, SSD_WIDTH),
                          ssd_norm_w[layer].reshape(1, SSD_WIDTH), e16, tri, shift,
                          gw, gla_gate_b[layer].reshape(1, GLA_KEY), gla_norm_w[layer].reshape(1, GLA_HEAD_V),
                          sel, msk, batch, seq)
        x2 = _outproj(x2, ya, yb, yc, w_out[layer].astype(BF16), final_norm_w.reshape(1, D_MODEL),
                      final=(layer == DEPTH - 1))
    return x2.reshape(batch, seq, D_MODEL).astype(x.dtype)
```

```python
import functools

import jax
import jax.numpy as jnp
import numpy as np
from jax import lax
from jax.experimental import pallas as pl
from jax.experimental.pallas import tpu as pltpu

F32 = jnp.float32
BF16 = jnp.bfloat16

D_MODEL = 1024
DEPTH = 2
D_INNER = 2 * D_MODEL
SB_WIDTH = D_INNER // 4
SB_HEAD_DIM = 64
SSD_WIDTH = D_INNER // 2
SSD_HEAD_DIM = 64
SSD_HEADS = SSD_WIDTH // SSD_HEAD_DIM
SSD_GROUPS = 2
SSD_STATE = 128
SSD_CONV = 4
SSD_CONV_DIM = SSD_WIDTH + 2 * SSD_GROUPS * SSD_STATE
GLA_WIDTH = D_INNER // 4
GLA_HEADS = 4
GLA_KEY = GLA_WIDTH // 2
GLA_HEAD_K = GLA_KEY // GLA_HEADS
GLA_HEAD_V = GLA_WIDTH // GLA_HEADS
GLA_GATE_RANK = 16
GLA_GATE_TAU = 16.0
NORM_EPS = 1e-6

LANES = 128
SUBLANES = 8
CHUNK = 128
MIX_SUBCHUNKS = 8
N_MAIN = 4 * SB_WIDTH + SSD_WIDTH + SSD_CONV_DIM + 2 * GLA_KEY + 2 * GLA_WIDTH
OFF_SSD_Z = 4 * SB_WIDTH
OFF_SSD_XBC = OFF_SSD_Z + SSD_WIDTH
OFF_GLA_Q = OFF_SSD_XBC + SSD_CONV_DIM
OFF_GLA_K = OFF_GLA_Q + GLA_KEY
OFF_GLA_V = OFF_GLA_K + GLA_KEY
OFF_GLA_Z = OFF_GLA_V + GLA_WIDTH
GLA_LEVELS = 7
GLA_DIRECT_MAX = 1e30
SB_TQ = 512
SB_NEAR_EXTRA = 2
LOG2E = 1.4426950408889634
VMEM_LIMIT = 56 * 1024 * 1024


def _dot(a, b):
    return jnp.dot(a, b, preferred_element_type=F32)


def _dot_nt(a, b):
    return lax.dot_general(a, b, (((1,), (1,)), ((), ())), preferred_element_type=F32)


def _dot_tn(a, b):
    return lax.dot_general(a, b, (((0,), (0,)), ((), ())), preferred_element_type=F32)


def _split2(x):
    hi = x.astype(BF16)
    lo = (x - hi.astype(F32)).astype(BF16)
    return hi, lo


def _split3(x):
    hi = x.astype(BF16)
    r = x - hi.astype(F32)
    mid = r.astype(BF16)
    lo = (r - mid.astype(F32)).astype(BF16)
    return hi, mid, lo


def _silu(x):
    hx = 0.5 * x
    return hx * jnp.tanh(hx) + hx


def _log_sigmoid(x):
    return jnp.minimum(x, 0.0) - jnp.log(1.0 + jnp.exp(-jnp.abs(x)))


def _softplus(x):
    return jnp.maximum(x, 0.0) + jnp.log(1.0 + jnp.exp(-jnp.abs(x)))


def _inproj_kernel(x_ref, nw_ref, wa_ref, wb_ref, ws_ref, o_ref, os_ref, *, n_chunk):
    x = x_ref[...]
    ms = jnp.mean(x * x, axis=-1, keepdims=True)
    h = (x * lax.rsqrt(ms + NORM_EPS) * nw_ref[...]).astype(BF16)
    for c0 in range(0, OFF_GLA_Q, n_chunk):
        o_ref[:, c0:c0 + n_chunk] = _dot(h, wa_ref[:, c0:c0 + n_chunk]).astype(BF16)
    for c0 in range(0, N_MAIN - OFF_GLA_Q, n_chunk):
        o_ref[:, OFF_GLA_Q + c0:OFF_GLA_Q + c0 + n_chunk] = _dot(h, wb_ref[:, c0:c0 + n_chunk]).astype(BF16)
    os_ref[...] = _dot(h, ws_ref[...])


def _inproj(x2, nw, wa, wb, ws, *, tm=512, n_chunk=1536):
    t = x2.shape[0]
    wspec = lambda cols: pl.BlockSpec((D_MODEL, cols), lambda i: (0, 0))
    return pl.pallas_call(
        functools.partial(_inproj_kernel, n_chunk=n_chunk),
        grid=(t // tm,),
        in_specs=[
            pl.BlockSpec((tm, D_MODEL), lambda i: (i, 0)),
            pl.BlockSpec((1, D_MODEL), lambda i: (0, 0)),
            wspec(OFF_GLA_Q), wspec(N_MAIN - OFF_GLA_Q), wspec(LANES),
        ],
        out_specs=[
            pl.BlockSpec((tm, N_MAIN), lambda i: (i, 0)),
            pl.BlockSpec((tm, LANES), lambda i: (i, 0)),
        ],
        out_shape=[
            jax.ShapeDtypeStruct((t, N_MAIN), BF16),
            jax.ShapeDtypeStruct((t, LANES), F32),
        ],
        compiler_params=pltpu.CompilerParams(
            dimension_semantics=("arbitrary",), vmem_limit_bytes=VMEM_LIMIT),
        name="inproj",
    )(x2, nw, wa, wb, ws)


def _sb_kernel(*refs, seq, far):
    if far:
        q_ref, k_ref, v_ref, z_ref, yin_ref, car_ref, o_ref = refs[:7]
    else:
        q_ref, k_ref, v_ref, z_ref, o_ref, car_ref = refs[:6]
    k0_s, k1_s, vt_s, acc_s, zt0_s, zt1_s, a0_s, a1_s, qs_s = refs[-9:]
    zt_s = (zt0_s, zt1_s)
    a_s = (a0_s, a1_s)
    kb_n = LANES
    n_kb = seq // kb_n
    groups = kb_n // SUBLANES
    hd = SB_HEAD_DIM
    tq = SB_TQ
    U = tq // kb_n

    lane = lax.broadcasted_iota(jnp.int32, (kb_n, kb_n), 1)
    row = lax.broadcasted_iota(jnp.int32, (kb_n, kb_n), 0)
    src = (row % SUBLANES) * groups + row // SUBLANES
    perm = jnp.where(lane == src, 1.0, 0.0).astype(BF16)
    src_t = (lane % SUBLANES) * groups + lane // SUBLANES
    perm_t = jnp.where(row == src_t, 1.0, 0.0).astype(BF16)
    lo_lanes = lane < hd

    def prep(n, carry):
        for j in range(U):
            kb = n * U + j
            r0 = pl.multiple_of(kb * kb_n, kb_n)
            kp = _dot(perm, k_ref[pl.ds(r0, kb_n), :]).astype(BF16)
            k0_s[kb] = jnp.where(lo_lanes, kp, jnp.zeros_like(kp))
            k1_s[kb] = jnp.where(lo_lanes, jnp.zeros_like(kp), kp)
            vt = v_ref[pl.ds(r0, kb_n), :].astype(F32).T.astype(BF16)
            vt_s[kb] = _dot(vt, perm_t).astype(BF16)
            qs_s[pl.ds(r0, kb_n), :] = (q_ref[pl.ds(r0, kb_n), :].astype(F32) * (0.5 * hd ** -0.5)).astype(BF16)
        return carry

    lax.fori_loop(0, n_kb // U, prep, 0)

    def block_scan(zt, base):
        nl = zt.shape[1]
        seg = lax.broadcasted_iota(jnp.int32, (SUBLANES, nl), 0)
        run = jnp.ones((SUBLANES, nl), F32)
        s_rows = [None] * groups
        for i in reversed(range(groups)):
            zh = zt[i * SUBLANES:(i + 1) * SUBLANES, :]
            beta = 0.5 * jnp.tanh(zh) + 0.5
            if base is not None:
                beta = jnp.where(i < base, beta, 0.0)
            s_rows[i] = beta * run
            run = run - s_rows[i]
        inc = run
        for sh in (1, 2, 4):
            inc = inc * jnp.where(seg < SUBLANES - sh, pltpu.roll(inc, SUBLANES - sh, 0), 1.0)
        excl = jnp.where(seg < SUBLANES - 1, pltpu.roll(inc, SUBLANES - 1, 0), 1.0)
        total = jnp.broadcast_to(inc[0:1, :], (SUBLANES, nl))
        return s_rows, excl, total

    def chunk(ztc, base, cc):
        s_rows, excl, total = block_scan(ztc, base)
        off = excl * cc
        a = jnp.concatenate([s_rows[i] * off for i in range(groups)], axis=0)
        return a, cc * total

    seg_d = lax.broadcasted_iota(jnp.int32, (SUBLANES, kb_n), 0)
    lane_d = lax.broadcasted_iota(jnp.int32, (SUBLANES, kb_n), 1)
    base_d = lane_d - seg_d * groups

    n_q = seq // tq

    def issue_scores(item, slot):
        qi, kb0, _, nb = item
        qb = qs_s[qi * tq:(qi + 1) * tq, :]
        k_cat = jnp.concatenate([k0_s[kb0 + j] for j in range(nb)] + [k1_s[kb0 + j] for j in range(nb)], axis=0)
        zt_s[slot][0:2 * nb * kb_n, :] = _dot_nt(k_cat, qb)

    def add_values(item, slot):
        _, kb0, _, nb = item
        vts = [vt_s[kb0 + j] for j in range(nb)]
        for h in range(2):
            v_cat = jnp.concatenate([vts[j][h * hd:(h + 1) * hd, :] for j in range(nb)], axis=1)
            acc_s[h] += _dot(v_cat, a_s[slot][h, 0:nb * kb_n, :])

    def finish_qblock(qi):
        q0 = qi * tq
        o = jnp.concatenate([acc_s[0], acc_s[1]], axis=0).T
        zg = z_ref[pl.ds(q0, tq), :].astype(F32)
        y = o * _silu(zg)
        if far:
            y = y + yin_ref[pl.ds(q0, tq), :].astype(F32)
        o_ref[pl.ds(q0, tq), :] = y.astype(BF16)
        acc_s[...] = jnp.zeros_like(acc_s)

    def weights(slot, carries, item):
        _, _, diag, nb = item
        out = []
        for h in range(2):
            carry = carries[h]
            for j in reversed(range(nb)):
                r0 = (h * nb + j) * kb_n
                if not diag:
                    pieces = [chunk(zt_s[slot][r0:r0 + kb_n, l:l + kb_n], None, carry[:, l:l + kb_n])
                              for l in range(0, tq, kb_n)]
                    a = jnp.concatenate([p[0] for p in pieces], axis=1)
                    carry = jnp.concatenate([p[1] for p in pieces], axis=1)
                else:
                    l0 = j * kb_n
                    parts, cparts = [], []
                    if l0 > 0:
                        parts.append(jnp.zeros((kb_n, l0), F32))
                        cparts.append(carry[:, 0:l0])
                    a_m, c_m = chunk(zt_s[slot][r0:r0 + kb_n, l0:l0 + kb_n], base_d, carry[:, l0:l0 + kb_n])
                    parts.append(a_m)
                    cparts.append(c_m)
                    if l0 + kb_n < tq:
                        a_u, c_u = chunk(zt_s[slot][r0:r0 + kb_n, l0 + kb_n:tq], None, carry[:, l0 + kb_n:tq])
                        parts.append(a_u)
                        cparts.append(c_u)
                    a = jnp.concatenate(parts, axis=1)
                    carry = jnp.concatenate(cparts, axis=1)
                a_s[slot][h, j * kb_n:(j + 1) * kb_n, :] = a.astype(BF16)
            out.append(carry)
        return tuple(out)

    items = []
    for qi in range(n_q):
        if far:
            if qi > 0:
                items.append((qi, (qi - 1) * U, False, U - SB_NEAR_EXTRA))
            items.extend((qi, j * U, False, U) for j in reversed(range(qi - 1)))
        else:
            items.append((qi, qi * U, True, U))
            if qi > 0:
                items.append((qi, qi * U - SB_NEAR_EXTRA, False, SB_NEAR_EXTRA))
    first_item = {}
    for n, it in enumerate(items):
        first_item.setdefault(it[0], n)
    if far:
        for qi in range(n_q):
            if qi not in first_item:
                q0 = qi * tq
                o_ref[pl.ds(q0, tq), :] = yin_ref[pl.ds(q0, tq), :]

    acc_s[...] = jnp.zeros_like(acc_s)
    issue_scores(items[0], 0)
    carries = None
    for n, item in enumerate(items):
        qi = item[0]
        if n + 1 < len(items):
            issue_scores(items[n + 1], (n + 1) % 2)
        if n > 0:
            add_values(items[n - 1], (n - 1) % 2)
        if first_item[qi] == n:
            if n > 0:
                finish_qblock(items[n - 1][0])
            if far:
                carries = (car_ref[qi, 0], car_ref[qi, 1])
            else:
                one = jnp.ones((SUBLANES, tq), F32)
                carries = (one, one)
        carries = weights(n % 2, carries, item)
        if not far and (n + 1 == len(items) or items[n + 1][0] != qi):
            car_ref[qi, 0] = carries[0]
            car_ref[qi, 1] = carries[1]
    add_values(items[-1], (len(items) - 1) % 2)
    finish_qblock(items[-1][0])


def _sb_call(proj, batch, seq, far, extra):
    pairs = SB_WIDTH // LANES
    n_kb = seq // LANES
    n_q = seq // SB_TQ
    blk = lambda off: pl.BlockSpec((seq, LANES), lambda b, p, off=off: (b, off + p))
    y_spec = pl.BlockSpec((seq, LANES), lambda b, p: (b, p))
    car_spec = pl.BlockSpec((None, None, n_q, 2, SUBLANES, SB_TQ), lambda b, p: (b, p, 0, 0, 0, 0))
    y_shape = jax.ShapeDtypeStruct((batch * seq, SB_WIDTH), BF16)
    car_shape = jax.ShapeDtypeStruct((batch, pairs, n_q, 2, SUBLANES, SB_TQ), F32)
    return pl.pallas_call(
        functools.partial(_sb_kernel, seq=seq, far=far),
        grid=(batch, pairs),
        in_specs=[blk(0), blk(pairs), blk(2 * pairs), blk(3 * pairs)] + ([y_spec, car_spec] if far else []),
        out_specs=y_spec if far else [y_spec, car_spec],
        out_shape=y_shape if far else [y_shape, car_shape],
        scratch_shapes=[
            pltpu.VMEM((n_kb, LANES, LANES), BF16),
            pltpu.VMEM((n_kb, LANES, LANES), BF16),
            pltpu.VMEM((n_kb, LANES, LANES), BF16),
            pltpu.VMEM((2, SB_HEAD_DIM, SB_TQ), F32),
            pltpu.VMEM((2 * SB_TQ, SB_TQ), F32),
            pltpu.VMEM((2 * SB_TQ, SB_TQ), F32),
            pltpu.VMEM((2, SB_TQ, SB_TQ), BF16),
            pltpu.VMEM((2, SB_TQ, SB_TQ), BF16),
            pltpu.VMEM((seq, LANES), BF16),
        ],
        compiler_params=pltpu.CompilerParams(
            dimension_semantics=("arbitrary", "arbitrary"), vmem_limit_bytes=VMEM_LIMIT),
        name="sb_far" if far else "sb_near",
    )(proj, proj, proj, proj, *extra)


def _sb_attention(proj, batch, seq):
    ya, car = _sb_call(proj, batch, seq, False, ())
    if seq // SB_TQ <= 1:
        return ya
    alive = jnp.max(car[:, :, 1:]) > 0.0
    return lax.cond(alive, lambda: _sb_call(proj, batch, seq, True, (ya, car)), lambda: ya)


def _mix_kernel(z_ref, xbc_ref, xprev_ref, sm_ref, cw_ref, cb_ref, dtb_ref, alog_ref, dexp_ref, snw_ref,
                e16_ref, tri_ref, shift_ref, q_ref, k_ref, v_ref, gz_ref, gw_ref, gb_ref, gnw_ref,
                sel_ref, msk_ref, ob_ref, oc_ref, h_s, st_s, att_s):
    first = pl.program_id(1) == 0

    @pl.when(first)
    def _():
        h_s[...] = jnp.zeros_like(h_s)
        st_s[...] = jnp.zeros_like(st_s)

    gla = _gla_phases(q_ref, k_ref, v_ref, gz_ref, sm_ref, gw_ref, gb_ref, gnw_ref, tri_ref, sel_ref,
                      msk_ref, oc_ref, st_s, att_s)
    next(gla)
    staged = [_ssd_stage1(sub, first, xbc_ref, xprev_ref, sm_ref, cw_ref, cb_ref, dtb_ref, alog_ref,
                          e16_ref, tri_ref, shift_ref) for sub in range(MIX_SUBCHUNKS)]
    next(gla)
    for sub in range(MIX_SUBCHUNKS):
        _ssd_stage2(sub, staged[sub], z_ref, dexp_ref, snw_ref, ob_ref, h_s)
    for _ in gla:
        pass


def _ssd_stage1(sub, first, xbc_ref, xprev_ref, sm_ref, cw_ref, cb_ref, dtb_ref, alog_ref,
                e16_ref, tri_ref, shift_ref):
    L = CHUNK
    rows = slice(sub * L, (sub + 1) * L)

    cur = xbc_ref[rows, :]
    if sub == 0:
        prev = xprev_ref[...]
        prev = jnp.where(first, jnp.zeros_like(prev), prev)
    else:
        prev = xbc_ref[(sub - 1) * L:sub * L, :]
    both = jnp.concatenate([prev, cur], axis=0)
    shifted = _dot(shift_ref[...], both)
    acc = cb_ref[...] + cw_ref[SSD_CONV - 1:SSD_CONV, :] * cur.astype(F32)
    for j in range(SSD_CONV - 1):
        acc = acc + cw_ref[j:j + 1, :] * shifted[j * L:(j + 1) * L, :]
    xbc = _silu(acc)
    xs = xbc[:, 0:SSD_WIDTH]

    lane = lax.broadcasted_iota(jnp.int32, (L, LANES), 1)
    head_lanes = lane < SSD_HEADS
    dt = jnp.where(head_lanes, _softplus(sm_ref[rows, :] + dtb_ref[...]), 0.0)
    a = dt * (-jnp.exp(alog_ref[...]))
    tri = tri_ref[...]
    a1, a2, a3 = _split3(a)
    cs = _dot(tri, a1) + _dot(tri, a2) + _dot(tri, a3)
    ecs = jnp.exp(cs)
    dend = jnp.exp(cs[L - 1:L, :] - cs)
    stack = jnp.concatenate([dt, dt * dend, ecs], axis=0)
    s_hi, s_lo = _split2(stack)
    e16 = e16_ref[...]
    ex = _dot(s_hi, e16) + _dot(s_lo, e16)
    x_dt = (xs * ex[0:L, :]).astype(BF16)
    x_st = (xs * ex[L:2 * L, :]).astype(BF16)
    ecs_x = ex[2 * L:3 * L, :]
    cs2 = cs * LOG2E
    return xbc, xs, cs2, cs2.T, x_dt, x_st, ecs_x


def _ssd_stage2(sub, staged, z_ref, dexp_ref, nw_ref, o_ref, h_s):
    xbc, xs, cs, cs_t, x_dt, x_st, ecs_x = staged
    L = CHUNK
    gw = SSD_WIDTH // SSD_GROUPS
    hpg = SSD_HEADS // SSD_GROUPS
    n = SSD_STATE
    rows = slice(sub * L, (sub + 1) * L)

    rowi = lax.broadcasted_iota(jnp.int32, (L, L), 0)
    coli = lax.broadcasted_iota(jnp.int32, (L, L), 1)
    causal = rowi >= coli
    lo_lanes = coli < SSD_HEAD_DIM

    y_parts = []
    for g in range(SSD_GROUPS):
        bm = xbc[:, SSD_WIDTH + g * n:SSD_WIDTH + (g + 1) * n].astype(BF16)
        cm = xbc[:, SSD_WIDTH + SSD_GROUPS * n + g * n:SSD_WIDTH + SSD_GROUPS * n + (g + 1) * n].astype(BF16)
        scores = _dot_nt(cm, bm)
        for pr in range(hpg // 2):
            col = g * gw + pr * LANES
            xp = x_dt[:, col:col + LANES]
            yd = None
            for half in range(2):
                h = g * hpg + 2 * pr + half
                sg = cs[:, h:h + 1] - cs_t[h:h + 1, :]
                dec = jnp.where(causal, jnp.exp2(sg), 0.0)
                w = (scores * dec).astype(BF16)
                xm = jnp.where(lo_lanes, xp, jnp.zeros_like(xp)) if half == 0 else jnp.where(
                    lo_lanes, jnp.zeros_like(xp), xp)
                t = _dot(w, xm)
                yd = t if yd is None else yd + t
            y_parts.append(yd)
        h_prev = h_s[g]
        y_off = _dot(cm, h_prev.astype(BF16)) * ecs_x[:, g * gw:(g + 1) * gw]
        y_parts.append(y_off)
        st_new = _dot_tn(bm, x_st[:, g * gw:(g + 1) * gw])
        h_s[g] = h_prev * ecs_x[L - 1:L, g * gw:(g + 1) * gw] + st_new

    per_g = 1 + hpg // 2
    outs = []
    zg = z_ref[rows, :].astype(F32)
    for g in range(SSD_GROUPS):
        yd = jnp.concatenate(y_parts[g * per_g:g * per_g + hpg // 2], axis=1)
        y = yd + y_parts[g * per_g + hpg // 2] + dexp_ref[:, g * gw:(g + 1) * gw] * xs[:, g * gw:(g + 1) * gw]
        zz = zg[:, g * gw:(g + 1) * gw]
        y = y * _silu(zz)
        ms = jnp.mean(y * y, axis=-1, keepdims=True)
        outs.append(y * lax.rsqrt(ms + NORM_EPS) * nw_ref[:, g * gw:(g + 1) * gw])
    o_ref[rows, :] = jnp.concatenate(outs, axis=1).astype(BF16)


def _gla_phases(q_ref, k_ref, v_ref, z_ref, sm_ref, gw_ref, gb_ref, nw_ref, tri_ref, sel_ref,
                msk_ref, o_ref, st_s, att_s):
    L = CHUNK
    kk = GLA_HEAD_K
    vv = GLA_HEAD_V

    lane = lax.broadcasted_iota(jnp.int32, (L, GLA_KEY), 1)
    rowi = lax.broadcasted_iota(jnp.int32, (L, GLA_KEY), 0)
    head_of_lane = lane // kk

    def head_mask(xb, h):
        return jnp.where(head_of_lane == h, xb, jnp.zeros_like(xb))

    g_hi, g_lo = _split2(gw_ref[...])
    tri = tri_ref[...]
    subs = []
    for sub in range(MIX_SUBCHUNKS):
        rows = slice(sub * L, (sub + 1) * L)
        s_hi, s_lo = _split2(sm_ref[rows, :])
        u = _dot(s_hi, g_hi) + _dot(s_lo, g_hi) + _dot(s_hi, g_lo) + gb_ref[...]
        lg = _log_sigmoid(u) * (1.0 / GLA_GATE_TAU)
        l1, l2, l3 = _split3(lg)
        gc = _dot(tri, l1) + _dot(tri, l2) + _dot(tri, l3)
        q = q_ref[rows, :].astype(F32) * (kk ** -0.5)
        k = k_ref[rows, :].astype(F32)
        g_last = gc[L - 1:L, :]
        q_in = (q * jnp.exp(gc)).astype(BF16)
        k_dec = (k * jnp.exp(g_last - gc)).astype(BF16)
        k_growth = jnp.max(jnp.abs(k), axis=0, keepdims=True) * jnp.exp(-g_last)
        subs.append(dict(rows=rows, gc=gc, q=q, k=k, g_last=g_last, q_in=q_in, k_dec=k_dec, growth=k_growth))

    growth = subs[0]["growth"]
    for s in subs[1:]:
        growth = jnp.maximum(growth, s["growth"])
    direct = jnp.max(growth) < GLA_DIRECT_MAX
    yield

    @pl.when(direct)
    def _():
        m = msk_ref[GLA_LEVELS + 1] > 0.0
        for i, s in enumerate(subs):
            k_up = (s["k"] * jnp.exp(-s["gc"])).astype(BF16)
            for h in range(GLA_HEADS):
                att_s[i, h] = jnp.where(m, _dot_nt(head_mask(s["q_in"], h), k_up), 0.0)

    @pl.when(jnp.logical_not(direct))
    def _():
        sel = sel_ref[...]
        for i, s in enumerate(subs):
            gc, q, k = s["gc"], s["q"], s["k"]
            c1, c2, c3 = _split3(gc)
            ref_g = _dot(sel, c1) + _dot(sel, c2) + _dot(sel, c3)
            att = [None] * GLA_HEADS
            for lv in range(GLA_LEVELS):
                expo = -jnp.abs(gc - ref_g[lv * L:(lv + 1) * L, :])
                upper = ((rowi >> lv) & 1) == 1
                xb = (jnp.where(upper, q, k) * jnp.exp(expo)).astype(BF16)
                m = msk_ref[lv]
                for h in range(GLA_HEADS):
                    p = _dot_nt(head_mask(xb, h), xb) * m
                    att[h] = p if att[h] is None else att[h] + p
            qb = q.astype(BF16)
            kb = k.astype(BF16)
            m = msk_ref[GLA_LEVELS]
            for h in range(GLA_HEADS):
                att_s[i, h] = att[h] + _dot_nt(head_mask(qb, h), kb) * m

    yield
    lane_s = lax.broadcasted_iota(jnp.int32, (vv, GLA_KEY), 1) // kk
    st = st_s[...]
    for i, s in enumerate(subs):
        rows = s["rows"]
        st_b = st.astype(BF16)
        st_new = st * jnp.exp(s["g_last"])
        outs = []
        for h in range(GLA_HEADS):
            vh = v_ref[rows, h * vv:(h + 1) * vv]
            o = _dot_nt(head_mask(s["q_in"], h), st_b) + _dot(att_s[i, h].astype(BF16), vh)
            upd = _dot_tn(vh, s["k_dec"])
            st_new = st_new + jnp.where(lane_s == h, upd, 0.0)
            ms = jnp.mean(o * o, axis=-1, keepdims=True)
            o = o * lax.rsqrt(ms + NORM_EPS) * nw_ref[...]
            zz = z_ref[rows, h * vv:(h + 1) * vv].astype(F32)
            outs.append(o * _silu(zz))
        o_ref[rows, :] = jnp.concatenate(outs, axis=1).astype(BF16)
        st = st_new
    st_s[...] = st


def _ssd_gla(proj, small, cw, cb, dtb, alog, dexp, snw, e16, tri, shift, gw, gb, gnw, sel, msk, batch, seq):
    step = MIX_SUBCHUNKS * CHUNK
    nc = seq // step
    row = lambda b, c: b * nc + c
    prev_row = lambda b, c: b * (seq // CHUNK) + jnp.maximum(c * MIX_SUBCHUNKS - 1, 0)
    const = lambda shape: pl.BlockSpec(shape, lambda b, c: (0,) * len(shape))
    rows = lambda width, off: pl.BlockSpec((step, width), lambda b, c: (row(b, c), off // width))
    return pl.pallas_call(
        _mix_kernel,
        grid=(batch, nc),
        in_specs=[
            rows(SSD_WIDTH, OFF_SSD_Z), rows(SSD_CONV_DIM, OFF_SSD_XBC),
            pl.BlockSpec((CHUNK, SSD_CONV_DIM), lambda b, c: (prev_row(b, c), OFF_SSD_XBC // SSD_CONV_DIM)),
            rows(LANES, 0),
            const((SSD_CONV, SSD_CONV_DIM)), const((1, SSD_CONV_DIM)), const((1, LANES)), const((1, LANES)),
            const((1, SSD_WIDTH)), const((1, SSD_WIDTH)), const((LANES, SSD_WIDTH)), const((CHUNK, CHUNK)),
            const(((SSD_CONV - 1) * CHUNK, 2 * CHUNK)),
            rows(GLA_KEY, OFF_GLA_Q), rows(GLA_KEY, OFF_GLA_K), rows(GLA_WIDTH, OFF_GLA_V), rows(GLA_WIDTH, OFF_GLA_Z),
            const((LANES, GLA_KEY)), const((1, GLA_KEY)), const((1, GLA_HEAD_V)),
            const((GLA_LEVELS * CHUNK, CHUNK)), const((GLA_LEVELS + 2, CHUNK, CHUNK)),
        ],
        out_specs=[pl.BlockSpec((step, SSD_WIDTH), lambda b, c: (row(b, c), 0)),
                   pl.BlockSpec((step, GLA_WIDTH), lambda b, c: (row(b, c), 0))],
        out_shape=[jax.ShapeDtypeStruct((batch * seq, SSD_WIDTH), BF16),
                   jax.ShapeDtypeStruct((batch * seq, GLA_WIDTH), BF16)],
        scratch_shapes=[pltpu.VMEM((SSD_GROUPS, SSD_STATE, SSD_WIDTH // SSD_GROUPS), F32),
                        pltpu.VMEM((GLA_HEAD_V, GLA_KEY), F32),
                        pltpu.VMEM((MIX_SUBCHUNKS, GLA_HEADS, CHUNK, CHUNK), F32)],
        compiler_params=pltpu.CompilerParams(
            dimension_semantics=("arbitrary", "arbitrary"), vmem_limit_bytes=VMEM_LIMIT),
        name="ssd_gla",
    )(proj, proj, proj, small, cw, cb, dtb, alog, dexp, snw, e16, tri, shift,
      proj, proj, proj, proj, gw, gb, gnw, sel, msk)


def _outproj_kernel(x_ref, ya_ref, yb_ref, yc_ref, w_ref, fw_ref, o_ref, *, final):
    y = _dot(ya_ref[...], w_ref[0:SB_WIDTH, :])
    y = y + _dot(yb_ref[...], w_ref[SB_WIDTH:SB_WIDTH + SSD_WIDTH, :])
    y = y + _dot(yc_ref[...], w_ref[SB_WIDTH + SSD_WIDTH:D_INNER, :])
    xn = x_ref[...] + y
    if final:
        ms = jnp.mean(xn * xn, axis=-1, keepdims=True)
        xn = xn * lax.rsqrt(ms + NORM_EPS) * fw_ref[...]
    o_ref[...] = xn


def _outproj(x2, ya, yb, yc, w, fw, *, final, tm=512):
    t = x2.shape[0]
    rows = lambda width: pl.BlockSpec((tm, width), lambda i: (i, 0))
    return pl.pallas_call(
        functools.partial(_outproj_kernel, final=final),
        grid=(t // tm,),
        in_specs=[rows(D_MODEL), rows(SB_WIDTH), rows(SSD_WIDTH), rows(GLA_WIDTH),
                  pl.BlockSpec((D_INNER, D_MODEL), lambda i: (0, 0)),
                  pl.BlockSpec((1, D_MODEL), lambda i: (0, 0))],
        out_specs=rows(D_MODEL),
        out_shape=jax.ShapeDtypeStruct((t, D_MODEL), F32),
        compiler_params=pltpu.CompilerParams(
            dimension_semantics=("arbitrary",), vmem_limit_bytes=VMEM_LIMIT),
        name="outproj_final" if final else "outproj",
    )(x2, ya, yb, yc, w, fw)


def _constants():
    L = CHUNK
    t = np.arange(L)
    tri = (t[:, None] >= t[None, :]).astype(np.float32)
    sel = np.zeros((GLA_LEVELS * L, L), np.float32)
    msk = np.zeros((GLA_LEVELS + 2, L, L), np.float32)
    msk[GLA_LEVELS + 1] = tri
    shift = np.zeros(((SSD_CONV - 1) * L, 2 * L), np.float32)
    for j in range(SSD_CONV - 1):
        shift[j * L + t, L + t - (SSD_CONV - 1 - j)] = 1.0
    for lv in range(GLA_LEVELS):
        mid = ((t >> (lv + 1)) << (lv + 1)) + (1 << lv)
        sel[lv * L + t, mid - 1] = 1.0
        same = (t[:, None] >> (lv + 1)) == (t[None, :] >> (lv + 1))
        up = ((t[:, None] >> lv) & 1) == 1
        low = ((t[None, :] >> lv) & 1) == 0
        msk[lv] = (same & up & low).astype(np.float32)
    msk[GLA_LEVELS] = np.eye(L, dtype=np.float32)
    e16 = np.zeros((LANES, SSD_WIDTH), np.float32)
    for h in range(SSD_HEADS):
        e16[h, h * SSD_HEAD_DIM:(h + 1) * SSD_HEAD_DIM] = 1.0
    return (jnp.asarray(tri, BF16), jnp.asarray(sel, BF16), jnp.asarray(msk, F32), jnp.asarray(e16, BF16),
            jnp.asarray(shift, BF16))


def _pad_lanes(v, offset=0):
    out = jnp.zeros((1, LANES), F32)
    return out.at[0, offset:offset + v.shape[0]].set(v.astype(F32))


def kernel(x, norm_w, w_in, ssd_conv_w, ssd_conv_b, ssd_dt_bias, ssd_a_log, ssd_d, ssd_norm_w,
           gla_gate_w, gla_gate_b, gla_norm_w, w_out, final_norm_w):
    batch, seq, _ = x.shape
    assert seq % 256 == 0 and (batch * seq) % 512 == 0
    tri, sel, msk, e16, shift = _constants()
    x2 = x.reshape(batch * seq, D_MODEL).astype(F32)
    dt_col = OFF_GLA_Q
    glr_col = dt_col + SSD_HEADS + 2 * GLA_KEY + 2 * GLA_WIDTH
    for layer in range(DEPTH):
        w = w_in[layer]
        wa = w[:, :dt_col].astype(BF16)
        wb = w[:, dt_col + SSD_HEADS:glr_col].astype(BF16)
        ws = jnp.concatenate(
            [w[:, dt_col:dt_col + SSD_HEADS], w[:, glr_col:glr_col + GLA_GATE_RANK],
             jnp.zeros((D_MODEL, LANES - SSD_HEADS - GLA_GATE_RANK), w.dtype)], axis=1).astype(BF16)
        proj, small = _inproj(x2, norm_w[layer].reshape(1, D_MODEL), wa, wb, ws)

        ya = _sb_attention(proj, batch, seq)
        gw = jnp.zeros((LANES, GLA_KEY), F32).at[SSD_HEADS:SSD_HEADS + GLA_GATE_RANK, :].set(gla_gate_w[layer])
        yb, yc = _ssd_gla(proj, small, ssd_conv_w[layer], ssd_conv_b[layer].reshape(1, SSD_CONV_DIM),
                          _pad_lanes(ssd_dt_bias[layer]), _pad_lanes(ssd_a_log[layer]),
                          jnp.repeat(ssd_d[layer], SSD_HEAD_DIM).reshape(1, SSD_WIDTH),
                          ssd_norm_w[layer].reshape(1, SSD_WIDTH), e16, tri, shift,
                          gw, gla_gate_b[layer].reshape(1, GLA_KEY), gla_norm_w[layer].reshape(1, GLA_HEAD_V),
                          sel, msk, batch, seq)
        x2 = _outproj(x2, ya, yb, yc, w_out[layer].astype(BF16), final_norm_w.reshape(1, D_MODEL),
                      final=(layer == DEPTH - 1))
    return x2.reshape(batch, seq, D_MODEL).astype(x.dtype)
```

```python
import functools

import jax
import jax.numpy as jnp
import numpy as np
from jax import lax
from jax.experimental import pallas as pl
from jax.experimental.pallas import tpu as pltpu

F32 = jnp.float32
BF16 = jnp.bfloat16

D_MODEL = 1024
DEPTH = 2
D_INNER = 2 * D_MODEL
SB_WIDTH = D_INNER // 4
SB_HEAD_DIM = 64
SSD_WIDTH = D_INNER // 2
SSD_HEAD_DIM = 64
SSD_HEADS = SSD_WIDTH // SSD_HEAD_DIM
SSD_GROUPS = 2
SSD_STATE = 128
SSD_CONV = 4
SSD_CONV_DIM = SSD_WIDTH + 2 * SSD_GROUPS * SSD_STATE
GLA_WIDTH = D_INNER // 4
GLA_HEADS = 4
GLA_KEY = GLA_WIDTH // 2
GLA_HEAD_K = GLA_KEY // GLA_HEADS
GLA_HEAD_V = GLA_WIDTH // GLA_HEADS
GLA_GATE_RANK = 16
GLA_GATE_TAU = 16.0
NORM_EPS = 1e-6

LANES = 128
SUBLANES = 8
CHUNK = 128
MIX_SUBCHUNKS = 4
N_MAIN = 4 * SB_WIDTH + SSD_WIDTH + SSD_CONV_DIM + 2 * GLA_KEY + 2 * GLA_WIDTH
OFF_SSD_Z = 4 * SB_WIDTH
OFF_SSD_XBC = OFF_SSD_Z + SSD_WIDTH
OFF_GLA_Q = OFF_SSD_XBC + SSD_CONV_DIM
OFF_GLA_K = OFF_GLA_Q + GLA_KEY
OFF_GLA_V = OFF_GLA_K + GLA_KEY
OFF_GLA_Z = OFF_GLA_V + GLA_WIDTH
GLA_LEVELS = 7
GLA_DIRECT_MAX = 1e30
SB_TQ = 512
SB_NEAR_EXTRA = 2
LOG2E = 1.4426950408889634
VMEM_LIMIT = 56 * 1024 * 1024


def _dot(a, b):
    return jnp.dot(a, b, preferred_element_type=F32)


def _dot_nt(a, b):
    return lax.dot_general(a, b, (((1,), (1,)), ((), ())), preferred_element_type=F32)


def _dot_tn(a, b):
    return lax.dot_general(a, b, (((0,), (0,)), ((), ())), preferred_element_type=F32)


def _split2(x):
    hi = x.astype(BF16)
    lo = (x - hi.astype(F32)).astype(BF16)
    return hi, lo


def _split3(x):
    hi = x.astype(BF16)
    r = x - hi.astype(F32)
    mid = r.astype(BF16)
    lo = (r - mid.astype(F32)).astype(BF16)
    return hi, mid, lo


def _silu(x):
    hx = 0.5 * x
    return hx * jnp.tanh(hx) + hx


def _log_sigmoid(x):
    return jnp.minimum(x, 0.0) - jnp.log(1.0 + jnp.exp(-jnp.abs(x)))


def _softplus(x):
    return jnp.maximum(x, 0.0) + jnp.log(1.0 + jnp.exp(-jnp.abs(x)))


def _inproj_kernel(x_ref, nw_ref, wa_ref, wb_ref, ws_ref, o_ref, os_ref, *, n_chunk):
    x = x_ref[...]
    ms = jnp.mean(x * x, axis=-1, keepdims=True)
    h = (x * lax.rsqrt(ms + NORM_EPS) * nw_ref[...]).astype(BF16)
    for c0 in range(0, OFF_GLA_Q, n_chunk):
        o_ref[:, c0:c0 + n_chunk] = _dot(h, wa_ref[:, c0:c0 + n_chunk]).astype(BF16)
    for c0 in range(0, N_MAIN - OFF_GLA_Q, n_chunk):
        o_ref[:, OFF_GLA_Q + c0:OFF_GLA_Q + c0 + n_chunk] = _dot(h, wb_ref[:, c0:c0 + n_chunk]).astype(BF16)
    os_ref[...] = _dot(h, ws_ref[...])


def _inproj(x2, nw, wa, wb, ws, *, tm=512, n_chunk=1536):
    t = x2.shape[0]
    wspec = lambda cols: pl.BlockSpec((D_MODEL, cols), lambda i: (0, 0))
    return pl.pallas_call(
        functools.partial(_inproj_kernel, n_chunk=n_chunk),
        grid=(t // tm,),
        in_specs=[
            pl.BlockSpec((tm, D_MODEL), lambda i: (i, 0)),
            pl.BlockSpec((1, D_MODEL), lambda i: (0, 0)),
            wspec(OFF_GLA_Q), wspec(N_MAIN - OFF_GLA_Q), wspec(LANES),
        ],
        out_specs=[
            pl.BlockSpec((tm, N_MAIN), lambda i: (i, 0)),
            pl.BlockSpec((tm, LANES), lambda i: (i, 0)),
        ],
        out_shape=[
            jax.ShapeDtypeStruct((t, N_MAIN), BF16),
            jax.ShapeDtypeStruct((t, LANES), F32),
        ],
        compiler_params=pltpu.CompilerParams(
            dimension_semantics=("arbitrary",), vmem_limit_bytes=VMEM_LIMIT),
        name="inproj",
    )(x2, nw, wa, wb, ws)


def _sb_kernel(*refs, seq, far):
    if far:
        q_ref, k_ref, v_ref, z_ref, yin_ref, car_ref, o_ref = refs[:7]
    else:
        q_ref, k_ref, v_ref, z_ref, o_ref, car_ref = refs[:6]
    k0_s, k1_s, vt_s, acc_s, zt0_s, zt1_s, a0_s, a1_s, qs_s = refs[-9:]
    zt_s = (zt0_s, zt1_s)
    a_s = (a0_s, a1_s)
    kb_n = LANES
    n_kb = seq // kb_n
    groups = kb_n // SUBLANES
    hd = SB_HEAD_DIM
    tq = SB_TQ
    U = tq // kb_n

    lane = lax.broadcasted_iota(jnp.int32, (kb_n, kb_n), 1)
    row = lax.broadcasted_iota(jnp.int32, (kb_n, kb_n), 0)
    src = (row % SUBLANES) * groups + row // SUBLANES
    perm = jnp.where(lane == src, 1.0, 0.0).astype(BF16)
    src_t = (lane % SUBLANES) * groups + lane // SUBLANES
    perm_t = jnp.where(row == src_t, 1.0, 0.0).astype(BF16)
    lo_lanes = lane < hd

    def prep(n):
        for j in range(U):
            kb = n * U + j
            r0 = kb * kb_n
            kp = _dot(perm, k_ref[pl.ds(r0, kb_n), :]).astype(BF16)
            k0_s[kb] = jnp.where(lo_lanes, kp, jnp.zeros_like(kp))
            k1_s[kb] = jnp.where(lo_lanes, jnp.zeros_like(kp), kp)
            vt = v_ref[pl.ds(r0, kb_n), :].astype(F32).T.astype(BF16)
            vt_s[kb] = _dot(vt, perm_t).astype(BF16)
            qs_s[pl.ds(r0, kb_n), :] = (q_ref[pl.ds(r0, kb_n), :].astype(F32) * (0.5 * hd ** -0.5)).astype(BF16)

    for n in range(n_kb // U):
        prep(n)

    def block_scan(zt, base):
        nl = zt.shape[1]
        seg = lax.broadcasted_iota(jnp.int32, (SUBLANES, nl), 0)
        run = jnp.ones((SUBLANES, nl), F32)
        s_rows = [None] * groups
        for i in reversed(range(groups)):
            zh = zt[i * SUBLANES:(i + 1) * SUBLANES, :]
            beta = 0.5 * jnp.tanh(zh) + 0.5
            if base is not None:
                beta = jnp.where(i < base, beta, 0.0)
            s_rows[i] = beta * run
            run = run - s_rows[i]
        inc = run
        for sh in (1, 2, 4):
            inc = inc * jnp.where(seg < SUBLANES - sh, pltpu.roll(inc, SUBLANES - sh, 0), 1.0)
        excl = jnp.where(seg < SUBLANES - 1, pltpu.roll(inc, SUBLANES - 1, 0), 1.0)
        total = jnp.broadcast_to(inc[0:1, :], (SUBLANES, nl))
        return s_rows, excl, total

    def chunk(ztc, base, cc):
        s_rows, excl, total = block_scan(ztc, base)
        off = excl * cc
        a = jnp.concatenate([s_rows[i] * off for i in range(groups)], axis=0)
        return a, cc * total

    seg_d = lax.broadcasted_iota(jnp.int32, (SUBLANES, kb_n), 0)
    lane_d = lax.broadcasted_iota(jnp.int32, (SUBLANES, kb_n), 1)
    base_d = lane_d - seg_d * groups

    n_q = seq // tq

    def issue_scores(item, slot):
        qi, kb0, _, nb = item
        qb = qs_s[qi * tq:(qi + 1) * tq, :]
        k_cat = jnp.concatenate([k0_s[kb0 + j] for j in range(nb)] + [k1_s[kb0 + j] for j in range(nb)], axis=0)
        zt_s[slot][0:2 * nb * kb_n, :] = _dot_nt(k_cat, qb)

    def add_values(item, slot):
        _, kb0, _, nb = item
        vts = [vt_s[kb0 + j] for j in range(nb)]
        for h in range(2):
            v_cat = jnp.concatenate([vts[j][h * hd:(h + 1) * hd, :] for j in range(nb)], axis=1)
            acc_s[h] += _dot(v_cat, a_s[slot][h, 0:nb * kb_n, :])

    def finish_qblock(qi):
        q0 = qi * tq
        o = jnp.concatenate([acc_s[0], acc_s[1]], axis=0).T
        zg = z_ref[pl.ds(q0, tq), :].astype(F32)
        y = o * _silu(zg)
        if far:
            y = y + yin_ref[pl.ds(q0, tq), :].astype(F32)
        o_ref[pl.ds(q0, tq), :] = y.astype(BF16)
        acc_s[...] = jnp.zeros_like(acc_s)

    def weights(slot, carries, item):
        _, _, diag, nb = item
        out = []
        for h in range(2):
            carry = carries[h]
            for j in reversed(range(nb)):
                r0 = (h * nb + j) * kb_n
                if not diag:
                    pieces = [chunk(zt_s[slot][r0:r0 + kb_n, l:l + kb_n], None, carry[:, l:l + kb_n])
                              for l in range(0, tq, kb_n)]
                    a = jnp.concatenate([p[0] for p in pieces], axis=1)
                    carry = jnp.concatenate([p[1] for p in pieces], axis=1)
                else:
                    l0 = j * kb_n
                    parts, cparts = [], []
                    if l0 > 0:
                        parts.append(jnp.zeros((kb_n, l0), F32))
                        cparts.append(carry[:, 0:l0])
                    a_m, c_m = chunk(zt_s[slot][r0:r0 + kb_n, l0:l0 + kb_n], base_d, carry[:, l0:l0 + kb_n])
                    parts.append(a_m)
                    cparts.append(c_m)
                    if l0 + kb_n < tq:
                        a_u, c_u = chunk(zt_s[slot][r0:r0 + kb_n, l0 + kb_n:tq], None, carry[:, l0 + kb_n:tq])
                        parts.append(a_u)
                        cparts.append(c_u)
                    a = jnp.concatenate(parts, axis=1)
                    carry = jnp.concatenate(cparts, axis=1)
                a_s[slot][h, j * kb_n:(j + 1) * kb_n, :] = a.astype(BF16)
            out.append(carry)
        return tuple(out)

    items = []
    for qi in range(n_q):
        if far:
            if qi > 0:
                items.append((qi, (qi - 1) * U, False, U - SB_NEAR_EXTRA))
            items.extend((qi, j * U, False, U) for j in reversed(range(qi - 1)))
        else:
            items.append((qi, qi * U, True, U))
            if qi > 0:
                items.append((qi, qi * U - SB_NEAR_EXTRA, False, SB_NEAR_EXTRA))
    first_item = {}
    for n, it in enumerate(items):
        first_item.setdefault(it[0], n)
    if far:
        for qi in range(n_q):
            if qi not in first_item:
                q0 = qi * tq
                o_ref[pl.ds(q0, tq), :] = yin_ref[pl.ds(q0, tq), :]

    acc_s[...] = jnp.zeros_like(acc_s)
    issue_scores(items[0], 0)
    carries = None
    for n, item in enumerate(items):
        qi = item[0]
        if n + 1 < len(items):
            issue_scores(items[n + 1], (n + 1) % 2)
        if n > 0:
            add_values(items[n - 1], (n - 1) % 2)
        if first_item[qi] == n:
            if n > 0:
                finish_qblock(items[n - 1][0])
            if far:
                carries = (car_ref[qi, 0], car_ref[qi, 1])
            else:
                one = jnp.ones((SUBLANES, tq), F32)
                carries = (one, one)
        carries = weights(n % 2, carries, item)
        if not far and (n + 1 == len(items) or items[n + 1][0] != qi):
            car_ref[qi, 0] = carries[0]
            car_ref[qi, 1] = carries[1]
    add_values(items[-1], (len(items) - 1) % 2)
    finish_qblock(items[-1][0])


def _sb_call(proj, batch, seq, far, extra):
    pairs = SB_WIDTH // LANES
    n_kb = seq // LANES
    n_q = seq // SB_TQ
    blk = lambda off: pl.BlockSpec((seq, LANES), lambda b, p, off=off: (b, off + p))
    y_spec = pl.BlockSpec((seq, LANES), lambda b, p: (b, p))
    car_spec = pl.BlockSpec((None, None, n_q, 2, SUBLANES, SB_TQ), lambda b, p: (b, p, 0, 0, 0, 0))
    y_shape = jax.ShapeDtypeStruct((batch * seq, SB_WIDTH), BF16)
    car_shape = jax.ShapeDtypeStruct((batch, pairs, n_q, 2, SUBLANES, SB_TQ), F32)
    return pl.pallas_call(
        functools.partial(_sb_kernel, seq=seq, far=far),
        grid=(batch, pairs),
        in_specs=[blk(0), blk(pairs), blk(2 * pairs), blk(3 * pairs)] + ([y_spec, car_spec] if far else []),
        out_specs=y_spec if far else [y_spec, car_spec],
        out_shape=y_shape if far else [y_shape, car_shape],
        scratch_shapes=[
            pltpu.VMEM((n_kb, LANES, LANES), BF16),
            pltpu.VMEM((n_kb, LANES, LANES), BF16),
            pltpu.VMEM((n_kb, LANES, LANES), BF16),
            pltpu.VMEM((2, SB_HEAD_DIM, SB_TQ), F32),
            pltpu.VMEM((2 * SB_TQ, SB_TQ), F32),
            pltpu.VMEM((2 * SB_TQ, SB_TQ), F32),
            pltpu.VMEM((2, SB_TQ, SB_TQ), BF16),
            pltpu.VMEM((2, SB_TQ, SB_TQ), BF16),
            pltpu.VMEM((seq, LANES), BF16),
        ],
        compiler_params=pltpu.CompilerParams(
            dimension_semantics=("arbitrary", "arbitrary"), vmem_limit_bytes=VMEM_LIMIT),
        name="sb_far" if far else "sb_near",
    )(proj, proj, proj, proj, *extra)


def _sb_attention(proj, batch, seq):
    ya, car = _sb_call(proj, batch, seq, False, ())
    if seq // SB_TQ <= 1:
        return ya
    alive = jnp.max(car[:, :, 1:]) > 0.0
    return lax.cond(alive, lambda: _sb_call(proj, batch, seq, True, (ya, car)), lambda: ya)


def _mix_kernel(z_ref, xbc_ref, xprev_ref, sm_ref, cw_ref, cb_ref, dtb_ref, alog_ref, dexp_ref, snw_ref,
                e16_ref, tri_ref, shift_ref, q_ref, k_ref, v_ref, gz_ref, gw_ref, gb_ref, gnw_ref,
                sel_ref, msk_ref, ob_ref, oc_ref, h_s, st_s, att_s):
    first = pl.program_id(1) == 0

    @pl.when(first)
    def _():
        h_s[...] = jnp.zeros_like(h_s)
        st_s[...] = jnp.zeros_like(st_s)

    gla = _gla_phases(q_ref, k_ref, v_ref, gz_ref, sm_ref, gw_ref, gb_ref, gnw_ref, tri_ref, sel_ref,
                      msk_ref, oc_ref, st_s, att_s)
    next(gla)
    staged = [_ssd_stage1(sub, first, xbc_ref, xprev_ref, sm_ref, cw_ref, cb_ref, dtb_ref, alog_ref,
                          e16_ref, tri_ref, shift_ref) for sub in range(MIX_SUBCHUNKS)]
    next(gla)
    for sub in range(MIX_SUBCHUNKS):
        _ssd_stage2(sub, staged[sub], z_ref, dexp_ref, snw_ref, ob_ref, h_s)
    for _ in gla:
        pass


def _ssd_stage1(sub, first, xbc_ref, xprev_ref, sm_ref, cw_ref, cb_ref, dtb_ref, alog_ref,
                e16_ref, tri_ref, shift_ref):
    L = CHUNK
    rows = slice(sub * L, (sub + 1) * L)

    cur = xbc_ref[rows, :]
    if sub == 0:
        prev = xprev_ref[...]
        prev = jnp.where(first, jnp.zeros_like(prev), prev)
    else:
        prev = xbc_ref[(sub - 1) * L:sub * L, :]
    both = jnp.concatenate([prev, cur], axis=0)
    shifted = _dot(shift_ref[...], both)
    acc = cb_ref[...] + cw_ref[SSD_CONV - 1:SSD_CONV, :] * cur.astype(F32)
    for j in range(SSD_CONV - 1):
        acc = acc + cw_ref[j:j + 1, :] * shifted[j * L:(j + 1) * L, :]
    xbc = _silu(acc)
    xs = xbc[:, 0:SSD_WIDTH]

    lane = lax.broadcasted_iota(jnp.int32, (L, LANES), 1)
    head_lanes = lane < SSD_HEADS
    dt = jnp.where(head_lanes, _softplus(sm_ref[rows, :] + dtb_ref[...]), 0.0)
    a = dt * (-jnp.exp(alog_ref[...]))
    tri = tri_ref[...]
    a1, a2, a3 = _split3(a)
    cs = _dot(tri, a1) + _dot(tri, a2) + _dot(tri, a3)
    ecs = jnp.exp(cs)
    dend = jnp.exp(cs[L - 1:L, :] - cs)
    stack = jnp.concatenate([dt, dt * dend, ecs], axis=0)
    s_hi, s_lo = _split2(stack)
    e16 = e16_ref[...]
    ex = _dot(s_hi, e16) + _dot(s_lo, e16)
    x_dt = (xs * ex[0:L, :]).astype(BF16)
    x_st = (xs * ex[L:2 * L, :]).astype(BF16)
    ecs_x = ex[2 * L:3 * L, :]
    cs2 = cs * LOG2E
    return xbc, xs, cs2, cs2.T, x_dt, x_st, ecs_x


def _ssd_stage2(sub, staged, z_ref, dexp_ref, nw_ref, o_ref, h_s):
    xbc, xs, cs, cs_t, x_dt, x_st, ecs_x = staged
    L = CHUNK
    gw = SSD_WIDTH // SSD_GROUPS
    hpg = SSD_HEADS // SSD_GROUPS
    n = SSD_STATE
    rows = slice(sub * L, (sub + 1) * L)

    rowi = lax.broadcasted_iota(jnp.int32, (L, L), 0)
    coli = lax.broadcasted_iota(jnp.int32, (L, L), 1)
    causal = rowi >= coli
    lo_lanes = coli < SSD_HEAD_DIM

    y_parts = []
    for g in range(SSD_GROUPS):
        bm = xbc[:, SSD_WIDTH + g * n:SSD_WIDTH + (g + 1) * n].astype(BF16)
        cm = xbc[:, SSD_WIDTH + SSD_GROUPS * n + g * n:SSD_WIDTH + SSD_GROUPS * n + (g + 1) * n].astype(BF16)
        scores = _dot_nt(cm, bm)
        for pr in range(hpg // 2):
            col = g * gw + pr * LANES
            xp = x_dt[:, col:col + LANES]
            yd = None
            for half in range(2):
                h = g * hpg + 2 * pr + half
                sg = cs[:, h:h + 1] - cs_t[h:h + 1, :]
                dec = jnp.where(causal, jnp.exp2(sg), 0.0)
                w = (scores * dec).astype(BF16)
                xm = jnp.where(lo_lanes, xp, jnp.zeros_like(xp)) if half == 0 else jnp.where(
                    lo_lanes, jnp.zeros_like(xp), xp)
                t = _dot(w, xm)
                yd = t if yd is None else yd + t
            y_parts.append(yd)
        h_prev = h_s[g]
        y_off = _dot(cm, h_prev.astype(BF16)) * ecs_x[:, g * gw:(g + 1) * gw]
        y_parts.append(y_off)
        st_new = _dot_tn(bm, x_st[:, g * gw:(g + 1) * gw])
        h_s[g] = h_prev * ecs_x[L - 1:L, g * gw:(g + 1) * gw] + st_new

    per_g = 1 + hpg // 2
    outs = []
    zg = z_ref[rows, :].astype(F32)
    for g in range(SSD_GROUPS):
        yd = jnp.concatenate(y_parts[g * per_g:g * per_g + hpg // 2], axis=1)
        y = yd + y_parts[g * per_g + hpg // 2] + dexp_ref[:, g * gw:(g + 1) * gw] * xs[:, g * gw:(g + 1) * gw]
        zz = zg[:, g * gw:(g + 1) * gw]
        y = y * _silu(zz)
        ms = jnp.mean(y * y, axis=-1, keepdims=True)
        outs.append(y * lax.rsqrt(ms + NORM_EPS) * nw_ref[:, g * gw:(g + 1) * gw])
    o_ref[rows, :] = jnp.concatenate(outs, axis=1).astype(BF16)


def _gla_phases(q_ref, k_ref, v_ref, z_ref, sm_ref, gw_ref, gb_ref, nw_ref, tri_ref, sel_ref,
                msk_ref, o_ref, st_s, att_s):
    L = CHUNK
    kk = GLA_HEAD_K
    vv = GLA_HEAD_V

    lane = lax.broadcasted_iota(jnp.int32, (L, GLA_KEY), 1)
    rowi = lax.broadcasted_iota(jnp.int32, (L, GLA_KEY), 0)
    head_of_lane = lane // kk

    def head_mask(xb, h):
        return jnp.where(head_of_lane == h, xb, jnp.zeros_like(xb))

    g_hi, g_lo = _split2(gw_ref[...])
    tri = tri_ref[...]
    subs = []
    for sub in range(MIX_SUBCHUNKS):
        rows = slice(sub * L, (sub + 1) * L)
        s_hi, s_lo = _split2(sm_ref[rows, :])
        u = _dot(s_hi, g_hi) + _dot(s_lo, g_hi) + _dot(s_hi, g_lo) + gb_ref[...]
        lg = _log_sigmoid(u) * (1.0 / GLA_GATE_TAU)
        l1, l2, l3 = _split3(lg)
        gc = _dot(tri, l1) + _dot(tri, l2) + _dot(tri, l3)
        q = q_ref[rows, :].astype(F32) * (kk ** -0.5)
        k = k_ref[rows, :].astype(F32)
        g_last = gc[L - 1:L, :]
        q_in = (q * jnp.exp(gc)).astype(BF16)
        k_dec = (k * jnp.exp(g_last - gc)).astype(BF16)
        k_growth = jnp.max(jnp.abs(k), axis=0, keepdims=True) * jnp.exp(-g_last)
        subs.append(dict(rows=rows, gc=gc, q=q, k=k, g_last=g_last, q_in=q_in, k_dec=k_dec, growth=k_growth))

    growth = subs[0]["growth"]
    for s in subs[1:]:
        growth = jnp.maximum(growth, s["growth"])
    direct = jnp.max(growth) < GLA_DIRECT_MAX
    yield

    @pl.when(direct)
    def _():
        m = msk_ref[GLA_LEVELS + 1] > 0.0
        for i, s in enumerate(subs):
            k_up = (s["k"] * jnp.exp(-s["gc"])).astype(BF16)
            for h in range(GLA_HEADS):
                att_s[i, h] = jnp.where(m, _dot_nt(head_mask(s["q_in"], h), k_up), 0.0)

    @pl.when(jnp.logical_not(direct))
    def _():
        sel = sel_ref[...]
        for i, s in enumerate(subs):
            gc, q, k = s["gc"], s["q"], s["k"]
            c1, c2, c3 = _split3(gc)
            ref_g = _dot(sel, c1) + _dot(sel, c2) + _dot(sel, c3)
            att = [None] * GLA_HEADS
            for lv in range(GLA_LEVELS):
                expo = -jnp.abs(gc - ref_g[lv * L:(lv + 1) * L, :])
                upper = ((rowi >> lv) & 1) == 1
                xb = (jnp.where(upper, q, k) * jnp.exp(expo)).astype(BF16)
                m = msk_ref[lv]
                for h in range(GLA_HEADS):
                    p = _dot_nt(head_mask(xb, h), xb) * m
                    att[h] = p if att[h] is None else att[h] + p
            qb = q.astype(BF16)
            kb = k.astype(BF16)
            m = msk_ref[GLA_LEVELS]
            for h in range(GLA_HEADS):
                att_s[i, h] = att[h] + _dot_nt(head_mask(qb, h), kb) * m

    yield
    lane_s = lax.broadcasted_iota(jnp.int32, (vv, GLA_KEY), 1) // kk
    st = st_s[...]
    for i, s in enumerate(subs):
        rows = s["rows"]
        st_b = st.astype(BF16)
        st_new = st * jnp.exp(s["g_last"])
        outs = []
        for h in range(GLA_HEADS):
            vh = v_ref[rows, h * vv:(h + 1) * vv]
            o = _dot_nt(head_mask(s["q_in"], h), st_b) + _dot(att_s[i, h].astype(BF16), vh)
            upd = _dot_tn(vh, s["k_dec"])
            st_new = st_new + jnp.where(lane_s == h, upd, 0.0)
            ms = jnp.mean(o * o, axis=-1, keepdims=True)
            o = o * lax.rsqrt(ms + NORM_EPS) * nw_ref[...]
            zz = z_ref[rows, h * vv:(h + 1) * vv].astype(F32)
            outs.append(o * _silu(zz))
        o_ref[rows, :] = jnp.concatenate(outs, axis=1).astype(BF16)
        st = st_new
    st_s[...] = st


def _ssd_gla(proj, small, cw, cb, dtb, alog, dexp, snw, e16, tri, shift, gw, gb, gnw, sel, msk, batch, seq):
    step = MIX_SUBCHUNKS * CHUNK
    nc = seq // step
    row = lambda b, c: b * nc + c
    prev_row = lambda b, c: b * (seq // CHUNK) + jnp.maximum(c * MIX_SUBCHUNKS - 1, 0)
    const = lambda shape: pl.BlockSpec(shape, lambda b, c: (0,) * len(shape))
    rows = lambda width, off: pl.BlockSpec((step, width), lambda b, c: (row(b, c), off // width))
    return pl.pallas_call(
        _mix_kernel,
        grid=(batch, nc),
        in_specs=[
            rows(SSD_WIDTH, OFF_SSD_Z), rows(SSD_CONV_DIM, OFF_SSD_XBC),
            pl.BlockSpec((CHUNK, SSD_CONV_DIM), lambda b, c: (prev_row(b, c), OFF_SSD_XBC // SSD_CONV_DIM)),
            rows(LANES, 0),
            const((SSD_CONV, SSD_CONV_DIM)), const((1, SSD_CONV_DIM)), const((1, LANES)), const((1, LANES)),
            const((1, SSD_WIDTH)), const((1, SSD_WIDTH)), const((LANES, SSD_WIDTH)), const((CHUNK, CHUNK)),
            const(((SSD_CONV - 1) * CHUNK, 2 * CHUNK)),
            rows(GLA_KEY, OFF_GLA_Q), rows(GLA_KEY, OFF_GLA_K), rows(GLA_WIDTH, OFF_GLA_V), rows(GLA_WIDTH, OFF_GLA_Z),
            const((LANES, GLA_KEY)), const((1, GLA_KEY)), const((1, GLA_HEAD_V)),
            const((GLA_LEVELS * CHUNK, CHUNK)), const((GLA_LEVELS + 2, CHUNK, CHUNK)),
        ],
        out_specs=[pl.BlockSpec((step, SSD_WIDTH), lambda b, c: (row(b, c), 0)),
                   pl.BlockSpec((step, GLA_WIDTH), lambda b, c: (row(b, c), 0))],
        out_shape=[jax.ShapeDtypeStruct((batch * seq, SSD_WIDTH), BF16),
                   jax.ShapeDtypeStruct((batch * seq, GLA_WIDTH), BF16)],
        scratch_shapes=[pltpu.VMEM((SSD_GROUPS, SSD_STATE, SSD_WIDTH // SSD_GROUPS), F32),
                        pltpu.VMEM((GLA_HEAD_V, GLA_KEY), F32),
                        pltpu.VMEM((MIX_SUBCHUNKS, GLA_HEADS, CHUNK, CHUNK), F32)],
        compiler_params=pltpu.CompilerParams(
            dimension_semantics=("arbitrary", "arbitrary"), vmem_limit_bytes=VMEM_LIMIT),
        name="ssd_gla",
    )(proj, proj, proj, small, cw, cb, dtb, alog, dexp, snw, e16, tri, shift,
      proj, proj, proj, proj, gw, gb, gnw, sel, msk)


def _outproj_kernel(x_ref, ya_ref, yb_ref, yc_ref, w_ref, fw_ref, o_ref, *, final):
    y = _dot(ya_ref[...], w_ref[0:SB_WIDTH, :])
    y = y + _dot(yb_ref[...], w_ref[SB_WIDTH:SB_WIDTH + SSD_WIDTH, :])
    y = y + _dot(yc_ref[...], w_ref[SB_WIDTH + SSD_WIDTH:D_INNER, :])
    xn = x_ref[...] + y
    if final:
        ms = jnp.mean(xn * xn, axis=-1, keepdims=True)
        xn = xn * lax.rsqrt(ms + NORM_EPS) * fw_ref[...]
    o_ref[...] = xn


def _outproj(x2, ya, yb, yc, w, fw, *, final, tm=512):
    t = x2.shape[0]
    rows = lambda width: pl.BlockSpec((tm, width), lambda i: (i, 0))
    return pl.pallas_call(
        functools.partial(_outproj_kernel, final=final),
        grid=(t // tm,),
        in_specs=[rows(D_MODEL), rows(SB_WIDTH), rows(SSD_WIDTH), rows(GLA_WIDTH),
                  pl.BlockSpec((D_INNER, D_MODEL), lambda i: (0, 0)),
                  pl.BlockSpec((1, D_MODEL), lambda i: (0, 0))],
        out_specs=rows(D_MODEL),
        out_shape=jax.ShapeDtypeStruct((t, D_MODEL), F32),
        compiler_params=pltpu.CompilerParams(
            dimension_semantics=("arbitrary",), vmem_limit_bytes=VMEM_LIMIT),
        name="outproj_final" if final else "outproj",
    )(x2, ya, yb, yc, w, fw)


def _constants():
    L = CHUNK
    t = np.arange(L)
    tri = (t[:, None] >= t[None, :]).astype(np.float32)
    sel = np.zeros((GLA_LEVELS * L, L), np.float32)
    msk = np.zeros((GLA_LEVELS + 2, L, L), np.float32)
    msk[GLA_LEVELS + 1] = tri
    shift = np.zeros(((SSD_CONV - 1) * L, 2 * L), np.float32)
    for j in range(SSD_CONV - 1):
        shift[j * L + t, L + t - (SSD_CONV - 1 - j)] = 1.0
    for lv in range(GLA_LEVELS):
        mid = ((t >> (lv + 1)) << (lv + 1)) + (1 << lv)
        sel[lv * L + t, mid - 1] = 1.0
        same = (t[:, None] >> (lv + 1)) == (t[None, :] >> (lv + 1))
        up = ((t[:, None] >> lv) & 1) == 1
        low = ((t[None, :] >> lv) & 1) == 0
        msk[lv] = (same & up & low).astype(np.float32)
    msk[GLA_LEVELS] = np.eye(L, dtype=np.float32)
    e16 = np.zeros((LANES, SSD_WIDTH), np.float32)
    for h in range(SSD_HEADS):
        e16[h, h * SSD_HEAD_DIM:(h + 1) * SSD_HEAD_DIM] = 1.0
    return (jnp.asarray(tri, BF16), jnp.asarray(sel, BF16), jnp.asarray(msk, F32), jnp.asarray(e16, BF16),
            jnp.asarray(shift, BF16))


def _pad_lanes(v, offset=0):
    out = jnp.zeros((1, LANES), F32)
    return out.at[0, offset:offset + v.shape[0]].set(v.astype(F32))


def kernel(x, norm_w, w_in, ssd_conv_w, ssd_conv_b, ssd_dt_bias, ssd_a_log, ssd_d, ssd_norm_w,
           gla_gate_w, gla_gate_b, gla_norm_w, w_out, final_norm_w):
    batch, seq, _ = x.shape
    assert seq % 256 == 0 and (batch * seq) % 512 == 0
    tri, sel, msk, e16, shift = _constants()
    x2 = x.reshape(batch * seq, D_MODEL).astype(F32)
    dt_col = OFF_GLA_Q
    glr_col = dt_col + SSD_HEADS + 2 * GLA_KEY + 2 * GLA_WIDTH
    for layer in range(DEPTH):
        w = w_in[layer]
        wa = w[:, :dt_col].astype(BF16)
        wb = w[:, dt_col + SSD_HEADS:glr_col].astype(BF16)
        ws = jnp.concatenate(
            [w[:, dt_col:dt_col + SSD_HEADS], w[:, glr_col:glr_col + GLA_GATE_RANK],
             jnp.zeros((D_MODEL, LANES - SSD_HEADS - GLA_GATE_RANK), w.dtype)], axis=1).astype(BF16)
        proj, small = _inproj(x2, norm_w[layer].reshape(1, D_MODEL), wa, wb, ws)

        ya = _sb_attention(proj, batch, seq)
        gw = jnp.zeros((LANES, GLA_KEY), F32).at[SSD_HEADS:SSD_HEADS + GLA_GATE_RANK, :].set(gla_gate_w[layer])
        yb, yc = _ssd_gla(proj, small, ssd_conv_w[layer], ssd_conv_b[layer].reshape(1, SSD_CONV_DIM),
                          _pad_lanes(ssd_dt_bias[layer]), _pad_lanes(ssd_a_log[layer]),
                          jnp.repeat(ssd_d[layer], SSD_HEAD_DIM).reshape(1, SSD_WIDTH),
                          ssd_norm_w[layer].reshape(1, SSD_WIDTH), e16, tri, shift,
                          gw, gla_gate_b[layer].reshape(1, GLA_KEY), gla_norm_w[layer].reshape(1, GLA_HEAD_V),
                          sel, msk, batch, seq)
        x2 = _outproj(x2, ya, yb, yc, w_out[layer].astype(BF16), final_norm_w.reshape(1, D_MODEL),
                      final=(layer == DEPTH - 1))
    return x2.reshape(batch, seq, D_MODEL).astype(x.dtype)
```
